```python
import jax, jax.numpy as jnp
from jax import lax
import numpy as np

D_MODEL = 1024
BATCH = 8
SEQ = 4096
DEPTH = 2
DEC_BATCH = 4
DEC_SEQ = 8192
PAST_LEN = 128

HEAD_DIM = 64
ROPE_THETA = 10000.0
EPS = 1e-6
NEG = -1e30
A_Q_HEADS = 8
A_KV_HEADS = 2
A_HALF_WINDOW = 128
B_HEADS = 4
B_CONFIGS = ((128, 1), (512, 4), (2048, 16))
N_BRANCH = len(B_CONFIGS)
A_Q_DIM = A_Q_HEADS * HEAD_DIM
A_KV_DIM = A_KV_HEADS * HEAD_DIM
B_DIM = B_HEADS * HEAD_DIM
IN0_DIM = A_Q_DIM + 2 * A_KV_DIM + N_BRANCH * 3 * B_DIM
OUT0_DIM = A_Q_DIM + B_DIM
C_HEADS = 16
C_NOPE = 64
C_ROPE = 32
C_QK = C_NOPE + C_ROPE
C_V = 64
C_Q_LORA = 256
C_KV_LORA = 256
IN1_DIM = C_Q_LORA + C_KV_LORA + C_ROPE
Q_BLOCK = 128
D_FF = 2816
CONV_W = 3
N_EVEN = (DEPTH + 1) // 2
N_ODD = DEPTH // 2

kernel_name = "hybrid_window_dilated_mla_encoder"


def rms_norm(x, g):
    xf = x.astype(jnp.float32)
    y = xf * lax.rsqrt(jnp.mean(xf * xf, axis=-1, keepdims=True) + EPS)
    return (y * g.astype(jnp.float32)).astype(x.dtype)


def rope(x):
    S, Dr = x.shape[1], x.shape[-1]
    inv = jnp.power(ROPE_THETA, -jnp.arange(0, Dr, 2, dtype=jnp.float32) / Dr)
    ang = jnp.arange(S, dtype=jnp.float32)[:, None] * inv[None, :]
    cos = jnp.cos(ang)[None, :, None, :]
    sin = jnp.sin(ang)[None, :, None, :]
    xf = x.astype(jnp.float32)
    x1, x2 = xf[..., :Dr // 2], xf[..., Dr // 2:]
    return jnp.concatenate([x1 * cos - x2 * sin, x2 * cos + x1 * sin], axis=-1).astype(x.dtype)


def local_attention(q, k, v, half, sink=None):
    Bt, L, Hq, Dh = q.shape
    Hkv = k.shape[2]
    G = Hq // Hkv
    blk = half
    nb = -(-L // blk)
    Lp = nb * blk
    qp = jnp.pad(q, ((0, 0), (0, Lp - L), (0, 0), (0, 0))).reshape(Bt, nb, blk, Hkv, G, Dh)
    kv_pad = ((0, 0), (blk, Lp - L + blk), (0, 0), (0, 0))
    kp = jnp.pad(k, kv_pad).reshape(Bt, nb + 2, blk, Hkv, Dh)
    vp = jnp.pad(v, kv_pad).reshape(Bt, nb + 2, blk, Hkv, Dh)
    kw = jnp.concatenate([kp[:, :-2], kp[:, 1:-1], kp[:, 2:]], axis=2)
    vw = jnp.concatenate([vp[:, :-2], vp[:, 1:-1], vp[:, 2:]], axis=2)
    qpos = jnp.arange(nb)[:, None] * blk + jnp.arange(blk)[None, :]
    kpos = (jnp.arange(nb)[:, None] - 1) * blk + jnp.arange(3 * blk)[None, :]
    kpos_b = kpos[:, None, :]
    mask = (jnp.abs(qpos[:, :, None] - kpos_b) <= half) & (kpos_b >= 0) & (kpos_b < L)
    s = jnp.einsum('bnqhgd,bnkhd->bnhgqk', qp, kw, preferred_element_type=jnp.float32) * (Dh ** -0.5)
    s = jnp.where(mask[None, :, None, None], s, NEG)
    m = jnp.max(s, axis=-1)
    if sink is not None:
        sk = sink.astype(jnp.float32).reshape(1, 1, Hkv, G, 1)
        m = jnp.maximum(m, sk)
    p = jnp.exp(s - m[..., None])
    den = jnp.sum(p, axis=-1)
    if sink is not None:
        den = den + jnp.exp(sk - m)
    o = jnp.einsum('bnhgqk,bnkhd->bnqhgd', p.astype(v.dtype), vw, preferred_element_type=jnp.float32)
    o = o / jnp.transpose(den, (0, 1, 4, 2, 3))[..., None]
    lse = jnp.transpose(m + jnp.log(den), (0, 1, 4, 2, 3)).reshape(Bt, Lp, Hq)[:, :L]
    o = o.reshape(Bt, Lp, Hq, Dh)[:, :L]
    return o.astype(q.dtype), lse


def dilated_attention(q, k, v, window, r):
    Bt, S, H, Dh = q.shape
    Ls = S // r

    def fold(t):
        return jnp.transpose(t.reshape(Bt, Ls, r, H, Dh), (0, 2, 1, 3, 4)).reshape(Bt * r, Ls, H, Dh)

    o, lse = local_attention(fold(q), fold(k), fold(v), (window // 2) // r)
    o = jnp.transpose(o.reshape(Bt, r, Ls, H, Dh), (0, 2, 1, 3, 4)).reshape(Bt, S, H, Dh)
    lse = jnp.transpose(lse.reshape(Bt, r, Ls, H), (0, 2, 1, 3)).reshape(Bt, S, H)
    return o, lse


def dense_attention(q, k, v):
    Bt, S, H, Dq = q.shape
    Dv = v.shape[-1]
    nq = S // Q_BLOCK
    qb = jnp.transpose(q.reshape(Bt, nq, Q_BLOCK, H, Dq), (1, 0, 2, 3, 4))
    scale = Dq ** -0.5

    def one_block(qi):
        s = jnp.einsum('bqhd,bkhd->bhqk', qi, k, preferred_element_type=jnp.float32) * scale
        p = jax.nn.softmax(s, axis=-1)
        o = jnp.einsum('bhqk,bkhd->bqhd', p.astype(v.dtype), v, preferred_element_type=jnp.float32)
        return o.astype(q.dtype)

    o = lax.map(one_block, qb)
    return jnp.transpose(o, (1, 0, 2, 3, 4)).reshape(Bt, S, H, Dv)


def hybrid_ab(h, w_in, a_q_gain, a_k_gain, a_sink, b_q_gain, b_k_gain, w_out):
    Bt, S, _ = h.shape
    proj = h @ w_in
    aq, ak, av, bqkv = jnp.split(proj, [A_Q_DIM, A_Q_DIM + A_KV_DIM, A_Q_DIM + 2 * A_KV_DIM], axis=-1)
    aq = rope(rms_norm(aq.reshape(Bt, S, A_Q_HEADS, HEAD_DIM), a_q_gain))
    ak = rope(rms_norm(ak.reshape(Bt, S, A_KV_HEADS, HEAD_DIM), a_k_gain))
    av = av.reshape(Bt, S, A_KV_HEADS, HEAD_DIM)
    a_out, _ = local_attention(aq, ak, av, A_HALF_WINDOW, a_sink)
    bqkv = bqkv.reshape(Bt, S, N_BRANCH, 3, B_HEADS, HEAD_DIM)
    outs, lses = [], []
    for g, (window, r) in enumerate(B_CONFIGS):
        bq = rope(rms_norm(bqkv[:, :, g, 0], b_q_gain[g]))
        bk = rope(rms_norm(bqkv[:, :, g, 1], b_k_gain[g]))
        o, lse = dilated_attention(bq, bk, bqkv[:, :, g, 2], window, r)
        outs.append(o)
        lses.append(lse)
    wts = jax.nn.softmax(jnp.stack(lses, axis=0), axis=0)
    b_out = jnp.sum(wts[..., None] * jnp.stack(outs, axis=0).astype(jnp.float32), axis=0).astype(h.dtype)
    cat = jnp.concatenate([a_out.reshape(Bt, S, A_Q_DIM), b_out.reshape(Bt, S, B_DIM)], axis=-1)
    return cat @ w_out


def mla(h, w_in, q_lora_gain, w_uq, kv_gain, w_ukv, q_gain, k_gain, w_out):
    Bt, S, _ = h.shape
    proj = h @ w_in
    cq, ckv, k_rope = jnp.split(proj, [C_Q_LORA, C_Q_LORA + C_KV_LORA], axis=-1)
    q = (rms_norm(cq, q_lora_gain) @ w_uq).reshape(Bt, S, C_HEADS, C_QK)
    kv = (rms_norm(ckv, kv_gain) @ w_ukv).reshape(Bt, S, C_HEADS, C_NOPE + C_V)
    k_nope, v = kv[..., :C_NOPE], kv[..., C_NOPE:]
    k = jnp.concatenate([k_nope, jnp.broadcast_to(k_rope[:, :, None, :], (Bt, S, C_HEADS, C_ROPE))], axis=-1)
    q = rms_norm(q, q_gain)
    k = rms_norm(k, k_gain)
    q = jnp.concatenate([q[..., :C_NOPE], rope(q[..., C_NOPE:])], axis=-1)
    k = jnp.concatenate([k[..., :C_NOPE], rope(k[..., C_NOPE:])], axis=-1)
    o = dense_attention(q, k, v)
    return o.reshape(Bt, S, C_HEADS * C_V) @ w_out


def conv_ffn(h, w_up, conv_w, conv_b, w_down):
    gate, val = jnp.split(h @ w_up, 2, axis=-1)
    gate = lax.conv_general_dilated(gate, conv_w[:, None, :].astype(gate.dtype), window_strides=(1,),
                                    padding='SAME', dimension_numbers=('NWC', 'WIO', 'NWC'),
                                    feature_group_count=D_FF) + conv_b
    return (jax.nn.gelu(gate) * val) @ w_down


def trunk(x, e_norm, e_w_in, e_a_q_gain, e_a_k_gain, e_a_sink, e_b_q_gain, e_b_k_gain, e_w_out,
          o_norm, o_w_in, o_q_lora_gain, o_w_uq, o_kv_gain, o_w_ukv, o_q_gain, o_k_gain, o_w_out,
          f_norm, f_w_up, f_conv_w, f_conv_b, f_w_down):
    for layer in range(DEPTH):
        i = layer // 2
        if layer % 2 == 0:
            x = x + hybrid_ab(rms_norm(x, e_norm[i]), e_w_in[i], e_a_q_gain[i], e_a_k_gain[i], e_a_sink[i],
                              e_b_q_gain[i], e_b_k_gain[i], e_w_out[i])
        else:
            x = x + mla(rms_norm(x, o_norm[i]), o_w_in[i], o_q_lora_gain[i], o_w_uq[i], o_kv_gain[i],
                        o_w_ukv[i], o_q_gain[i], o_k_gain[i], o_w_out[i])
        x = x + conv_ffn(rms_norm(x, f_norm[layer]), f_w_up[layer], f_conv_w[layer], f_conv_b[layer],
                         f_w_down[layer])
    return x


def setup_inputs(seed: int = 0) -> dict:
    key = jax.random.key(seed)
    ks = jax.random.split(key, 32)
    f32 = jnp.float32

    def w(k, shape, fan_in):
        return jax.random.normal(k, shape, f32) * (fan_in ** -0.5)

    def g(k, shape):
        return 1.0 + 0.02 * jax.random.normal(k, shape, f32)

    return {
        "x_prompt": jax.random.normal(ks[0], (BATCH, SEQ, D_MODEL), f32),
        "x_sample": jax.random.normal(ks[1], (DEC_BATCH, DEC_SEQ, D_MODEL), f32),
        "e_norm": g(ks[2], (N_EVEN, D_MODEL)),
        "e_w_in": w(ks[3], (N_EVEN, D_MODEL, IN0_DIM), D_MODEL),
        "e_a_q_gain": g(ks[4], (N_EVEN, HEAD_DIM)),
        "e_a_k_gain": g(ks[5], (N_EVEN, HEAD_DIM)),
        "e_a_sink": 0.5 * jax.random.normal(ks[6], (N_EVEN, A_Q_HEADS), f32),
        "e_b_q_gain": g(ks[7], (N_EVEN, N_BRANCH, HEAD_DIM)),
        "e_b_k_gain": g(ks[8], (N_EVEN, N_BRANCH, HEAD_DIM)),
        "e_w_out": w(ks[9], (N_EVEN, OUT0_DIM, D_MODEL), OUT0_DIM),
        "o_norm": g(ks[10], (N_ODD, D_MODEL)),
        "o_w_in": w(ks[11], (N_ODD, D_MODEL, IN1_DIM), D_MODEL),
        "o_q_lora_gain": g(ks[12], (N_ODD, C_Q_LORA)),
        "o_w_uq": w(ks[13], (N_ODD, C_Q_LORA, C_HEADS * C_QK), C_Q_LORA),
        "o_kv_gain": g(ks[14], (N_ODD, C_KV_LORA)),
        "o_w_ukv": w(ks[15], (N_ODD, C_KV_LORA, C_HEADS * (C_NOPE + C_V)), C_KV_LORA),
        "o_q_gain": g(ks[16], (N_ODD, C_QK)),
        "o_k_gain": g(ks[17], (N_ODD, C_QK)),
        "o_w_out": w(ks[18], (N_ODD, C_HEADS * C_V, D_MODEL), C_HEADS * C_V),
        "f_norm": g(ks[19], (DEPTH, D_MODEL)),
        "f_w_up": w(ks[20], (DEPTH, D_MODEL, 2 * D_FF), D_MODEL),
        "f_conv_w": w(ks[21], (DEPTH, CONV_W, D_FF), CONV_W),
        "f_conv_b": 0.02 * jax.random.normal(ks[22], (DEPTH, D_FF), f32),
        "f_w_down": w(ks[23], (DEPTH, D_FF, D_MODEL), D_FF),
    }


def reference(x_prompt, x_sample, e_norm, e_w_in, e_a_q_gain, e_a_k_gain, e_a_sink, e_b_q_gain, e_b_k_gain,
              e_w_out, o_norm, o_w_in, o_q_lora_gain, o_w_uq, o_kv_gain, o_w_ukv, o_q_gain, o_k_gain, o_w_out,
              f_norm, f_w_up, f_conv_w, f_conv_b, f_w_down):
    y_prompt = trunk(x_prompt, e_norm, e_w_in, e_a_q_gain, e_a_k_gain, e_a_sink, e_b_q_gain, e_b_k_gain, e_w_out,
                     o_norm, o_w_in, o_q_lora_gain, o_w_uq, o_kv_gain, o_w_ukv, o_q_gain, o_k_gain, o_w_out,
                     f_norm, f_w_up, f_conv_w, f_conv_b, f_w_down)
    y_sample = trunk(x_sample, e_norm, e_w_in, e_a_q_gain, e_a_k_gain, e_a_sink, e_b_q_gain, e_b_k_gain, e_w_out,
                     o_norm, o_w_in, o_q_lora_gain, o_w_uq, o_kv_gain, o_w_ukv, o_q_gain, o_k_gain, o_w_out,
                     f_norm, f_w_up, f_conv_w, f_conv_b, f_w_down)
    return (y_prompt, y_sample)
```

```python
import functools

import numpy as np
import jax
import jax.numpy as jnp
from jax import lax
from jax.experimental import pallas as pl
from jax.experimental.pallas import tpu as pltpu

D_MODEL = 1024
HEAD_DIM = 64
ROPE_THETA = 10000.0
EPS = 1e-6
NEG = -1e30
A_Q_HEADS = 8
A_KV_HEADS = 2
A_HALF_WINDOW = 128
B_HEADS = 4
B_CONFIGS = ((128, 1), (512, 4), (2048, 16))
N_BRANCH = len(B_CONFIGS)
A_Q_DIM = A_Q_HEADS * HEAD_DIM
A_KV_DIM = A_KV_HEADS * HEAD_DIM
B_DIM = B_HEADS * HEAD_DIM
C_HEADS = 16
C_NOPE = 64
C_ROPE = 32
C_QK = C_NOPE + C_ROPE
C_V = 64
C_Q_LORA = 256
C_KV_LORA = 256
D_FF = 2816
DEPTH = 2

LANES = 128
MXU_N = 256
BF16_ROWS = 16
VMEM_LIMIT = 56 * 1024 * 1024

TM = 512
ATT_TQ = 128
MLA_TQ = 512
MLA_TK = 512
FF_CHUNK = 256
N_FF_CHUNK = D_FF // FF_CHUNK
HALO = BF16_ROWS

NR_BLOCKS = 9
P0_COLS = 13 * MXU_N

_f32 = jnp.float32
_bf16 = jnp.bfloat16


def _const_spec(shape):
    nd = len(shape)
    return pl.BlockSpec(shape, lambda *_: (0,) * nd, pipeline_mode=pl.Buffered(1))


def _params(n_axes):
    return pltpu.CompilerParams(dimension_semantics=("arbitrary",) * n_axes,
                                vmem_limit_bytes=VMEM_LIMIT)


def _rms(x, g):
    y = x * lax.rsqrt(jnp.mean(x * x, axis=-1, keepdims=True) + EPS)
    return y * g


def _group_sumsq(a, gm):
    sq = a * a
    hi = sq.astype(_bf16)
    lo = (sq - hi.astype(_f32)).astype(_bf16)
    return (jnp.dot(hi, gm, preferred_element_type=_f32)
            + jnp.dot(lo, gm, preferred_element_type=_f32))


def _norm_rope(a, gm, gain, cos, sin, inv_dim):
    ss = _group_sumsq(a, gm)
    y = a * lax.rsqrt(ss * inv_dim + EPS) * gain
    halves = []
    for t in range(2):
        yt = y[:, t * LANES:(t + 1) * LANES]
        halves.append(yt * cos + pltpu.roll(yt, LANES // 2, 1) * sin)
    return halves


def _proj0_kernel(x_ref, g_ref, w_ref, hg_ref, gm_ref, cos_ref, sin_ref,
                  qa_ref, kd_ref, vd_ref,
                  bq0_ref, bk0_ref, bv0_ref, bq1_ref, bk1_ref, bv1_ref, bq2_ref, bk2_ref, bv2_ref,
                  h_scr, fold_scr):
    tm = x_ref.shape[0]
    h_scr[...] = _rms(x_ref[...], g_ref[...]).astype(_bf16)
    cos = cos_ref[...]
    sin = sin_ref[...]
    gm = gm_ref[...]

    def matmul_block(j):
        return jnp.dot(h_scr[...], w_ref[:, j * MXU_N:(j + 1) * MXU_N], preferred_element_type=_f32)

    def store_folded(out_ref, val, r):
        if r == 1:
            out_ref[...] = val.astype(_bf16)
            return
        for t in range(2):
            fold_scr[t] = val[:, t * LANES:(t + 1) * LANES]
        for c in range(r):
            for t in range(2):
                lo = c * MXU_N + t * LANES
                out_ref[:, lo:lo + LANES] = fold_scr[t, pl.ds(c, tm // r, stride=r), :].astype(_bf16)

    def nr_block(j):
        lo, hi = _norm_rope(matmul_block(j), gm, hg_ref[:, j * MXU_N:(j + 1) * MXU_N],
                            cos, sin, 1.0 / HEAD_DIM)
        return jnp.concatenate([lo, hi], axis=1)

    qa_ref[:, 0:MXU_N] = nr_block(0).astype(_bf16)
    qa_ref[:, MXU_N:2 * MXU_N] = nr_block(1).astype(_bf16)
    kd_ref[...] = nr_block(2).astype(_bf16)
    b_refs = ((bq0_ref, bk0_ref, bv0_ref), (bq1_ref, bk1_ref, bv1_ref), (bq2_ref, bk2_ref, bv2_ref))
    for g, (_, r) in enumerate(B_CONFIGS):
        store_folded(b_refs[g][0], nr_block(3 + 2 * g), r)
        store_folded(b_refs[g][1], nr_block(4 + 2 * g), r)
    vd_ref[...] = matmul_block(NR_BLOCKS).astype(_bf16)
    for g, (_, r) in enumerate(B_CONFIGS):
        store_folded(b_refs[g][2], matmul_block(NR_BLOCKS + 1 + g), r)


def _proj0(x, seq, norm_g, w, head_gain, gm, cos, sin):
    t_rows = x.shape[0]
    tm = TM
    n_pos_blocks = seq // tm
    row = lambda i: (i, 0)
    out_shapes = [jax.ShapeDtypeStruct((t_rows, A_Q_DIM), _bf16),
                  jax.ShapeDtypeStruct((t_rows, MXU_N), _bf16),
                  jax.ShapeDtypeStruct((t_rows, MXU_N), _bf16)]
    out_specs = [pl.BlockSpec((tm, A_Q_DIM), row), pl.BlockSpec((tm, MXU_N), row),
                 pl.BlockSpec((tm, MXU_N), row)]
    for _, r in B_CONFIGS:
        for _ in range(3):
            out_shapes.append(jax.ShapeDtypeStruct((t_rows // r, r * B_DIM), _bf16))
            out_specs.append(pl.BlockSpec((tm // r, r * B_DIM), row))
    return pl.pallas_call(
        _proj0_kernel,
        name="proj0",
        grid=(t_rows // tm,),
        in_specs=[pl.BlockSpec((tm, D_MODEL), row),
                  _const_spec((1, D_MODEL)),
                  _const_spec((D_MODEL, P0_COLS)),
                  _const_spec((1, NR_BLOCKS * MXU_N)),
                  _const_spec((MXU_N, MXU_N)),
                  pl.BlockSpec((tm, LANES), lambda i: (i % n_pos_blocks, 0)),
                  pl.BlockSpec((tm, LANES), lambda i: (i % n_pos_blocks, 0))],
        out_specs=out_specs,
        out_shape=out_shapes,
        scratch_shapes=[pltpu.VMEM((tm, D_MODEL), _bf16), pltpu.VMEM((2, tm, LANES), _f32)],
        compiler_params=_params(1),
    )(x, norm_g, w, head_gain, gm, cos, sin)


def _local_attn_kernel(sink_ref, q_ref, k_ref, v_ref, *out_refs, seq, tq, half, has_sink, want_lse):
    o_ref = out_refs[0]
    pair = pl.program_id(1)
    win = tq + 2 * half
    lane = lax.broadcasted_iota(jnp.int32, (1, LANES), 1)
    first_qk = ((lane // 32) % 2) == 0
    first_v = lane < HEAD_DIM
    row = lax.broadcasted_iota(jnp.int32, (2 * tq, 1), 0)
    top = row < tq
    qoff = jnp.where(top, row, row - tq)
    koff = lax.broadcasted_iota(jnp.int32, (1, win), 1)
    if has_sink:
        sink = jnp.where(top, sink_ref[pair, 0], sink_ref[pair, 1])

    def step(i, carry):
        qs = pl.multiple_of(i * tq, tq)
        ws = pl.multiple_of(jnp.clip(qs - half, 0, seq - win), half)
        q2 = q_ref[0, pl.ds(qs, tq), :]
        zero = jnp.zeros_like(q2)
        qq = jnp.concatenate([jnp.where(first_qk, q2, zero), jnp.where(first_qk, zero, q2)], axis=0)
        kw = k_ref[0, pl.ds(ws, win), :]
        vw = v_ref[0, pl.ds(ws, win), :]
        s = lax.dot_general(qq, kw, (((1,), (1,)), ((), ())), preferred_element_type=_f32)
        valid = jnp.abs((qs + qoff) - (ws + koff)) <= half
        s = jnp.where(valid, s, NEG)
        m = jnp.max(s, axis=-1, keepdims=True)
        if has_sink:
            m = jnp.maximum(m, sink)
        p = jnp.exp(s - m)
        den = jnp.sum(p, axis=-1, keepdims=True)
        if has_sink:
            den = den + jnp.exp(sink - m)
        pv = jnp.dot(p.astype(_bf16), vw, preferred_element_type=_f32)
        od = pv / den
        o_ref[0, pl.ds(qs, tq), :] = jnp.where(first_v, od[:tq], od[tq:]).astype(o_ref.dtype)
        if want_lse:
            lse = m + jnp.log(den)
            out_refs[1][0, pl.ds(qs, tq), :] = jnp.where(first_v, lse[:tq], lse[tq:])
        return carry

    lax.fori_loop(0, seq // tq, step, 0)


def _local_attn(q, k, v, sink, *, half, kv_of_pair, want_lse):
    bsz, seq, width = q.shape
    n_pairs = width // LANES
    tq = ATT_TQ
    assert seq % tq == 0 and seq >= tq + 2 * half
    has_sink = sink is not None
    if not has_sink:
        sink = jnp.zeros((1, 2), _f32)
    qmap = lambda b, p: (b, 0, p)
    kmap = lambda b, p: (b, 0, kv_of_pair(p))
    out_shape = [jax.ShapeDtypeStruct((bsz, seq, width), _bf16)]
    out_specs = [pl.BlockSpec((1, seq, LANES), qmap)]
    if want_lse:
        out_shape.append(jax.ShapeDtypeStruct((bsz, seq, width), _f32))
        out_specs.append(pl.BlockSpec((1, seq, LANES), qmap))
    return pl.pallas_call(
        functools.partial(_local_attn_kernel, seq=seq, tq=tq, half=half, has_sink=has_sink,
                          want_lse=want_lse),
        name="local_attn_h%d" % half,
        grid=(bsz, n_pairs),
        in_specs=[pl.BlockSpec(memory_space=pltpu.SMEM),
                  pl.BlockSpec((1, seq, LANES), qmap),
                  pl.BlockSpec((1, seq, LANES), kmap),
                  pl.BlockSpec((1, seq, LANES), kmap)],
        out_specs=out_specs,
        out_shape=out_shape,
        compiler_params=_params(2),
    )(sink, q, k, v)


def _out0_kernel(x_ref, a_ref, o0_ref, l0_ref, o1_ref, l1_ref, o2_ref, l2_ref, wa_ref, wb_ref,
                 y_ref, o_scr, l_scr):
    tm = x_ref.shape[0]

    def unfold(src_ref, dst_scr, r):
        if r == 1:
            return src_ref[...].astype(_f32)
        for c in range(r):
            for t in range(2):
                lo = c * MXU_N + t * LANES
                dst_scr[t, pl.ds(c, tm // r, stride=r), :] = src_ref[:, lo:lo + LANES].astype(_f32)
        return jnp.concatenate([dst_scr[0], dst_scr[1]], axis=1)

    outs, lses = [], []
    for (o_ref, l_ref), (_, r) in zip(((o0_ref, l0_ref), (o1_ref, l1_ref), (o2_ref, l2_ref)), B_CONFIGS):
        outs.append(unfold(o_ref, o_scr, r))
        lses.append(unfold(l_ref, l_scr, r))
    m = jnp.maximum(jnp.maximum(lses[0], lses[1]), lses[2])
    es = [jnp.exp(l - m) for l in lses]
    tot = es[0] + es[1] + es[2]
    b_out = (es[0] / tot) * outs[0] + (es[1] / tot) * outs[1] + (es[2] / tot) * outs[2]
    y = x_ref[...] + jnp.dot(a_ref[...], wa_ref[...], preferred_element_type=_f32)
    y_ref[...] = y + jnp.dot(b_out.astype(_bf16), wb_ref[...], preferred_element_type=_f32)


def _out0(x, a_out, b_outs, b_lses, wa, wb):
    t_rows = x.shape[0]
    tm = TM
    row = lambda i: (i, 0)
    in_specs = [pl.BlockSpec((tm, D_MODEL), row), pl.BlockSpec((tm, A_Q_DIM), row)]
    args = [x, a_out]
    for (_, r), o, l in zip(B_CONFIGS, b_outs, b_lses):
        in_specs += [pl.BlockSpec((tm // r, r * B_DIM), row)] * 2
        args += [o, l]
    in_specs += [_const_spec((A_Q_DIM, D_MODEL)), _const_spec((B_DIM, D_MODEL))]
    args += [wa, wb]
    return pl.pallas_call(
        _out0_kernel,
        name="out0",
        grid=(t_rows // tm,),
        in_specs=in_specs,
        out_specs=pl.BlockSpec((tm, D_MODEL), row),
        out_shape=jax.ShapeDtypeStruct((t_rows, D_MODEL), _f32),
        scratch_shapes=[pltpu.VMEM((2, tm, LANES), _f32), pltpu.VMEM((2, tm, LANES), _f32)],
        compiler_params=_params(1),
    )(*args)


def _ffn_kernel(xp_ref, x_ref, xn_ref, g_ref, wg_ref, wv_ref, cw_ref, wd_ref, y_ref,
                h_scr, gate_scr, acc_scr, *, tiles_per_seq):
    tm = x_ref.shape[0]
    i = pl.program_id(0)
    pos = i % tiles_per_seq
    keep_prev = jnp.where(pos == 0, 0.0, 1.0)
    keep_next = jnp.where(pos == tiles_per_seq - 1, 0.0, 1.0)
    g = g_ref[...]
    h_scr[0:HALO, :] = _rms(xp_ref[...], g).astype(_bf16)
    h_scr[HALO:HALO + tm, :] = _rms(x_ref[...], g).astype(_bf16)
    h_scr[HALO + tm:, :] = _rms(xn_ref[...], g).astype(_bf16)
    acc_scr[...] = jnp.zeros_like(acc_scr)

    def chunk(j, carry):
        gate_scr[...] = jnp.dot(h_scr[...], wg_ref[j], preferred_element_type=_f32)
        val = jnp.dot(h_scr[HALO:HALO + tm, :], wv_ref[j], preferred_element_type=_f32)
        gate_scr[HALO - 8:HALO, :] = gate_scr[HALO - 8:HALO, :] * keep_prev
        gate_scr[HALO + tm:HALO + tm + 8, :] = gate_scr[HALO + tm:HALO + tm + 8, :] * keep_next
        cw = cw_ref[j]
        conv = (cw[0:1] * gate_scr[HALO - 1:HALO - 1 + tm, :]
                + cw[1:2] * gate_scr[HALO:HALO + tm, :]
                + cw[2:3] * gate_scr[HALO + 1:HALO + 1 + tm, :]
                + cw[3:4])
        act = (jax.nn.gelu(conv) * val).astype(_bf16)
        acc_scr[...] += jnp.dot(act, wd_ref[j], preferred_element_type=_f32)
        return carry

    lax.fori_loop(0, N_FF_CHUNK, chunk, 0)
    y_ref[...] = x_ref[...] + acc_scr[...]


def _ffn(x, seq, norm_g, wg, wv, cw, wd):
    t_rows = x.shape[0]
    tm = TM
    per = tm // HALO
    n_halo_blocks = t_rows // HALO
    return pl.pallas_call(
        functools.partial(_ffn_kernel, tiles_per_seq=seq // tm),
        name="conv_ffn",
        grid=(t_rows // tm,),
        in_specs=[pl.BlockSpec((HALO, D_MODEL), lambda i: (jnp.maximum(i * per - 1, 0), 0)),
                  pl.BlockSpec((tm, D_MODEL), lambda i: (i, 0)),
                  pl.BlockSpec((HALO, D_MODEL), lambda i: (jnp.minimum((i + 1) * per, n_halo_blocks - 1), 0)),
                  _const_spec((1, D_MODEL)),
                  _const_spec((N_FF_CHUNK, D_MODEL, FF_CHUNK)),
                  _const_spec((N_FF_CHUNK, D_MODEL, FF_CHUNK)),
                  _const_spec((N_FF_CHUNK, 8, FF_CHUNK)),
                  _const_spec((N_FF_CHUNK, FF_CHUNK, D_MODEL))],
        out_specs=pl.BlockSpec((tm, D_MODEL), lambda i: (i, 0)),
        out_shape=jax.ShapeDtypeStruct((t_rows, D_MODEL), _f32),
        scratch_shapes=[pltpu.VMEM((tm + 2 * HALO, D_MODEL), _bf16),
                        pltpu.VMEM((tm + 2 * HALO, FF_CHUNK), _f32),
                        pltpu.VMEM((tm, D_MODEL), _f32)],
        compiler_params=_params(1),
    )(x, x, x, norm_g, wg, wv, cw, wd)


def _proj1_kernel(x_ref, g_ref, win_ref, qlg_ref, kvg_ref, wq_ref, wk_ref, wv_ref, qg_ref, kg_ref,
                  gm_ref, cos_ref, sin_ref, q_ref, k_ref, v_ref):
    h = _rms(x_ref[...], g_ref[...]).astype(_bf16)
    proj = jnp.dot(h, win_ref[...], preferred_element_type=_f32)
    cq = _rms(proj[:, :C_Q_LORA], qlg_ref[...]).astype(_bf16)
    ckv = _rms(proj[:, C_Q_LORA:C_Q_LORA + C_KV_LORA], kvg_ref[...]).astype(_bf16)
    k_rope = proj[:, C_Q_LORA + C_KV_LORA:]
    k_rope2 = jnp.concatenate([k_rope, k_rope], axis=1)
    cos = cos_ref[...]
    sin = sin_ref[...]
    gm = gm_ref[...]
    qg = jnp.concatenate([qg_ref[...]] * 2, axis=1)
    kg = jnp.concatenate([kg_ref[...]] * 2, axis=1)
    for b in range(C_HEADS // 2):
        cols = slice(b * MXU_N, (b + 1) * MXU_N)
        aq = jnp.dot(cq, wq_ref[:, cols], preferred_element_type=_f32)
        lo, hi = _norm_rope(aq, gm, qg, cos, sin, 1.0 / C_QK)
        q_ref[:, cols] = jnp.concatenate([lo, hi], axis=1).astype(_bf16)
        ak = jnp.dot(ckv, wk_ref[:, cols], preferred_element_type=_f32) + k_rope2
        lo, hi = _norm_rope(ak, gm, kg, cos, sin, 1.0 / C_QK)
        k_ref[:, cols] = jnp.concatenate([lo, hi], axis=1).astype(_bf16)
    v_ref[...] = jnp.dot(ckv, wv_ref[...], preferred_element_type=_f32).astype(_bf16)


def _proj1(x, seq, norm_g, win, qlg, kvg, wq, wk, wv, qg, kg, gm, cos, sin):
    t_rows = x.shape[0]
    tm = TM
    n_pos_blocks = seq // tm
    row = lambda i: (i, 0)
    qk_w = C_HEADS * LANES
    v_w = C_HEADS * C_V
    return pl.pallas_call(
        _proj1_kernel,
        name="proj1",
        grid=(t_rows // tm,),
        in_specs=[pl.BlockSpec((tm, D_MODEL), row),
                  _const_spec((1, D_MODEL)),
                  _const_spec(win.shape),
                  _const_spec((1, C_Q_LORA)), _const_spec((1, C_KV_LORA)),
                  _const_spec((C_Q_LORA, qk_w)), _const_spec((C_KV_LORA, qk_w)),
                  _const_spec((C_KV_LORA, v_w)),
                  _const_spec((1, LANES)), _const_spec((1, LANES)),
                  _const_spec((MXU_N, MXU_N)),
                  pl.BlockSpec((tm, LANES), lambda i: (i % n_pos_blocks, 0)),
                  pl.BlockSpec((tm, LANES), lambda i: (i % n_pos_blocks, 0))],
        out_specs=[pl.BlockSpec((tm, qk_w), row), pl.BlockSpec((tm, qk_w), row),
                   pl.BlockSpec((tm, v_w), row)],
        out_shape=[jax.ShapeDtypeStruct((t_rows, qk_w), _bf16),
                   jax.ShapeDtypeStruct((t_rows, qk_w), _bf16),
                   jax.ShapeDtypeStruct((t_rows, v_w), _bf16)],
        compiler_params=_params(1),
    )(x, norm_g, win, qlg, kvg, wq, wk, wv, qg, kg, gm, cos, sin)


def _mla_kernel(q_ref, k_ref, v_ref, o_ref, *, seq, tk):
    tq = q_ref.shape[1]
    lane = lax.broadcasted_iota(jnp.int32, (1, LANES), 1)
    qs = [q_ref[0, :, hh * LANES:(hh + 1) * LANES] for hh in range(2)]

    def step(j, carry):
        ks = pl.multiple_of(j * tk, tk)
        vt = v_ref[0, pl.ds(ks, tk), :]
        new = []
        for hh in range(2):
            m_old, l_old, acc = carry[hh]
            kt = k_ref[0, pl.ds(ks, tk), hh * LANES:(hh + 1) * LANES]
            s = lax.dot_general(qs[hh], kt, (((1,), (1,)), ((), ())), preferred_element_type=_f32)
            m_new = jnp.maximum(m_old, jnp.max(s, axis=-1, keepdims=True))
            alpha = jnp.exp(m_old - m_new)
            p = jnp.exp(s - m_new)
            l_new = alpha * l_old + jnp.sum(p, axis=-1, keepdims=True)
            acc = alpha * acc + jnp.dot(p.astype(_bf16), vt, preferred_element_type=_f32)
            new.append((m_new, l_new, acc))
        return tuple(new)

    init = tuple((jnp.full((tq, 1), -jnp.inf, _f32), jnp.zeros((tq, 1), _f32),
                  jnp.zeros((tq, LANES), _f32)) for _ in range(2))
    (_, l0, acc0), (_, l1, acc1) = lax.fori_loop(0, seq // tk, step, init)
    o_ref[0] = jnp.where(lane < C_V, acc0 / l0, acc1 / l1).astype(o_ref.dtype)


def _mla_attn(q, k, v):
    bsz, seq, _ = q.shape
    n_pairs = C_HEADS // 2
    tq, tk = MLA_TQ, MLA_TK
    return pl.pallas_call(
        functools.partial(_mla_kernel, seq=seq, tk=tk),
        name="mla_attn",
        grid=(bsz, n_pairs, seq // tq),
        in_specs=[pl.BlockSpec((1, tq, 2 * LANES), lambda b, p, i: (b, i, p)),
                  pl.BlockSpec((1, seq, 2 * LANES), lambda b, p, i: (b, 0, p)),
                  pl.BlockSpec((1, seq, LANES), lambda b, p, i: (b, 0, p))],
        out_specs=pl.BlockSpec((1, tq, LANES), lambda b, p, i: (b, i, p)),
        out_shape=jax.ShapeDtypeStruct((bsz, seq, C_HEADS * C_V), _bf16),
        compiler_params=_params(3),
    )(q, k, v)


def _out1_kernel(x_ref, o_ref, w_ref, y_ref):
    y_ref[...] = x_ref[...] + jnp.dot(o_ref[...], w_ref[...], preferred_element_type=_f32)


def _out1(x, o, w):
    t_rows = x.shape[0]
    tm = TM
    row = lambda i: (i, 0)
    return pl.pallas_call(
        _out1_kernel,
        name="out1",
        grid=(t_rows // tm,),
        in_specs=[pl.BlockSpec((tm, D_MODEL), row), pl.BlockSpec((tm, o.shape[1]), row),
                  _const_spec(w.shape)],
        out_specs=pl.BlockSpec((tm, D_MODEL), row),
        out_shape=jax.ShapeDtypeStruct((t_rows, D_MODEL), _f32),
        compiler_params=_params(1),
    )(x, o, w)


def _pair_cols(a, b):
    r = np.arange(HEAD_DIM // 2)
    return np.concatenate([a + r, b + r, a + HEAD_DIM // 2 + r, b + HEAD_DIM // 2 + r])


_PAIR_DIMS = _pair_cols(0, 0)


def _layer0_columns():
    nr, val = [], []
    for p in range(A_Q_HEADS // 2):
        nr.append(_pair_cols(2 * p * HEAD_DIM, (2 * p + 1) * HEAD_DIM))
    for h in range(A_KV_HEADS):
        nr.append(_pair_cols(A_Q_DIM + h * HEAD_DIM, A_Q_DIM + h * HEAD_DIM))
    b0 = A_Q_DIM + 2 * A_KV_DIM
    for g in range(N_BRANCH):
        for t in range(2):
            base = b0 + g * 3 * B_DIM + t * B_DIM
            for p in range(B_HEADS // 2):
                nr.append(_pair_cols(base + 2 * p * HEAD_DIM, base + (2 * p + 1) * HEAD_DIM))
    d = np.arange(HEAD_DIM)
    for h in range(A_KV_HEADS):
        base = A_Q_DIM + A_KV_DIM + h * HEAD_DIM
        val.append(np.concatenate([base + d, base + d]))
    for g in range(N_BRANCH):
        val.append(b0 + g * 3 * B_DIM + 2 * B_DIM + np.arange(B_DIM))
    return np.concatenate(nr + val)


def _layer0_gains(a_q_gain, a_k_gain, b_q_gain, b_k_gain):
    scale = HEAD_DIM ** -0.5
    aq = a_q_gain[_PAIR_DIMS] * scale
    ak = a_k_gain[_PAIR_DIMS]
    parts = [aq] * (A_Q_HEADS // 2) + [ak] * A_KV_HEADS
    for g in range(N_BRANCH):
        parts += [b_q_gain[g][_PAIR_DIMS] * scale] * (B_HEADS // 2)
        parts += [b_k_gain[g][_PAIR_DIMS]] * (B_HEADS // 2)
    return jnp.concatenate(parts)[None, :]


def _layer0_group_matrix():
    lane = np.arange(MXU_N)
    head = (lane // LANES) * 2 + (lane // 32) % 2
    return jnp.asarray(head[:, None] == head[None, :], _bf16)


def _rope_tables(seq, dim):
    inv = jnp.power(ROPE_THETA, -jnp.arange(0, dim, 2, dtype=_f32) / dim)
    ang = jnp.arange(seq, dtype=_f32)[:, None] * inv[None, :]
    return jnp.cos(ang), jnp.sin(ang)


def _layer0_rope(seq):
    cos, sin = _rope_tables(seq, HEAD_DIM)
    return (jnp.concatenate([cos, cos, cos, cos], axis=1),
            jnp.concatenate([-sin, -sin, sin, sin], axis=1))


_C_HALF = C_ROPE // 2
_C_LANE_DIM = np.full(LANES, -1)
_C_LANE_DIM[0:_C_HALF] = C_NOPE + np.arange(_C_HALF)
_C_LANE_DIM[_C_HALF:_C_HALF + 32] = np.arange(32)
_C_LANE_DIM[64:64 + _C_HALF] = C_NOPE + _C_HALF + np.arange(_C_HALF)
_C_LANE_DIM[64 + _C_HALF:64 + _C_HALF + 32] = 32 + np.arange(32)


def _place_heads(w, per_head, lane_dim):
    idx = (np.arange(C_HEADS)[:, None] * per_head + np.maximum(lane_dim, 0)[None, :]).reshape(-1)
    keep = np.tile(lane_dim >= 0, C_HEADS)
    return jnp.where(jnp.asarray(keep)[None, :], w[:, idx], 0.0)


def _layer1_rope(seq):
    cos, sin = _rope_tables(seq, C_ROPE)
    ones = jnp.ones((seq, 64 - _C_HALF), _f32)
    zeros = jnp.zeros((seq, 64 - _C_HALF), _f32)
    return (jnp.concatenate([cos, ones, cos, ones], axis=1),
            jnp.concatenate([-sin, zeros, sin, zeros], axis=1))


def _layer1_group_matrix():
    lane = np.arange(MXU_N)
    return jnp.asarray((lane[:, None] // LANES) == (lane[None, :] // LANES), _bf16)


def _trunk(x3, e_norm, e_w_in, e_a_q_gain, e_a_k_gain, e_a_sink, e_b_q_gain, e_b_k_gain, e_w_out,
           o_norm, o_w_in, o_q_lora_gain, o_w_uq, o_kv_gain, o_w_ukv, o_q_gain, o_k_gain, o_w_out,
           f_norm, f_w_up, f_conv_w, f_conv_b, f_w_down):
    bsz, seq, _ = x3.shape
    assert seq % TM == 0
    x = x3.reshape(bsz * seq, D_MODEL)
    for layer in range(DEPTH):
        i = layer // 2
        if layer % 2 == 0:
            w0 = e_w_in[i][:, _layer0_columns()].astype(_bf16)
            cos, sin = _layer0_rope(seq)
            outs = _proj0(x, seq, e_norm[i][None, :], w0,
                          _layer0_gains(e_a_q_gain[i], e_a_k_gain[i], e_b_q_gain[i], e_b_k_gain[i]),
                          _layer0_group_matrix(), cos, sin)
            qa, kd, vd = outs[:3]
            a_out = _local_attn(qa.reshape(bsz, seq, A_Q_DIM), kd.reshape(bsz, seq, MXU_N),
                                vd.reshape(bsz, seq, MXU_N), e_a_sink[i].reshape(A_Q_HEADS // 2, 2),
                                half=A_HALF_WINDOW, kv_of_pair=lambda p: p // 2, want_lse=False)[0]
            b_outs, b_lses = [], []
            for g, (window, r) in enumerate(B_CONFIGS):
                bq, bk, bv = (t.reshape(bsz, seq // r, r * B_DIM) for t in outs[3 + 3 * g:6 + 3 * g])
                o, lse = _local_attn(bq, bk, bv, None, half=(window // 2) // r,
                                     kv_of_pair=lambda p: p, want_lse=True)
                b_outs.append(o.reshape(bsz * seq // r, r * B_DIM))
                b_lses.append(lse.reshape(bsz * seq // r, r * B_DIM))
            w_out = e_w_out[i].astype(_bf16)
            x = _out0(x, a_out.reshape(bsz * seq, A_Q_DIM), b_outs, b_lses,
                      w_out[:A_Q_DIM], w_out[A_Q_DIM:])
        else:
            w_in = o_w_in[i]
            k_rope_cols = jnp.zeros((D_MODEL, LANES), _f32)
            k_rope_cols = k_rope_cols.at[:, 0:_C_HALF].set(w_in[:, C_Q_LORA + C_KV_LORA:C_Q_LORA + C_KV_LORA + _C_HALF])
            k_rope_cols = k_rope_cols.at[:, 64:64 + _C_HALF].set(w_in[:, C_Q_LORA + C_KV_LORA + _C_HALF:])
            win = jnp.concatenate([w_in[:, :C_Q_LORA + C_KV_LORA], k_rope_cols], axis=1).astype(_bf16)
            wq = _place_heads(o_w_uq[i], C_QK, _C_LANE_DIM).astype(_bf16)
            nope_only = np.where(_C_LANE_DIM < C_NOPE, _C_LANE_DIM, -1)
            wk = _place_heads(o_w_ukv[i], C_NOPE + C_V, nope_only).astype(_bf16)
            wv = o_w_ukv[i].reshape(C_KV_LORA, C_HEADS, C_NOPE + C_V)[:, :, C_NOPE:].reshape(
                C_KV_LORA, C_HEADS * C_V).astype(_bf16)
            lane_ok = jnp.asarray(_C_LANE_DIM >= 0)
            qg = jnp.where(lane_ok, o_q_gain[i][np.maximum(_C_LANE_DIM, 0)] * (C_QK ** -0.5), 0.0)[None, :]
            kg = jnp.where(lane_ok, o_k_gain[i][np.maximum(_C_LANE_DIM, 0)], 0.0)[None, :]
            cos, sin = _layer1_rope(seq)
            q, k, v = _proj1(x, seq, o_norm[i][None, :], win, o_q_lora_gain[i][None, :],
                             o_kv_gain[i][None, :], wq, wk, wv, qg, kg, _layer1_group_matrix(), cos, sin)
            o = _mla_attn(q.reshape(bsz, seq, -1), k.reshape(bsz, seq, -1), v.reshape(bsz, seq, -1))
            x = _out1(x, o.reshape(bsz * seq, C_HEADS * C_V), o_w_out[i].astype(_bf16))
        w_up = f_w_up[layer]
        wg = w_up[:, :D_FF].reshape(D_MODEL, N_FF_CHUNK, FF_CHUNK).transpose(1, 0, 2).astype(_bf16)
        wv_ = w_up[:, D_FF:].reshape(D_MODEL, N_FF_CHUNK, FF_CHUNK).transpose(1, 0, 2).astype(_bf16)
        cw = jnp.concatenate([f_conv_w[layer], f_conv_b[layer][None, :], jnp.zeros((4, D_FF), _f32)], axis=0)
        cw = cw.reshape(8, N_FF_CHUNK, FF_CHUNK).transpose(1, 0, 2)
        wd = f_w_down[layer].reshape(N_FF_CHUNK, FF_CHUNK, D_MODEL).astype(_bf16)
        x = _ffn(x, seq, f_norm[layer][None, :], wg, wv_, cw, wd)
    return x.reshape(bsz, seq, D_MODEL)


def kernel(x_prompt, x_sample, e_norm, e_w_in, e_a_q_gain, e_a_k_gain, e_a_sink, e_b_q_gain, e_b_k_gain, e_w_out, o_norm, o_w_in, o_q_lora_gain, o_w_uq, o_kv_gain, o_w_ukv, o_q_gain, o_k_gain, o_w_out, f_norm, f_w_up, f_conv_w, f_conv_b, f_w_down):
    weights = (e_norm, e_w_in, e_a_q_gain, e_a_k_gain, e_a_sink, e_b_q_gain, e_b_k_gain, e_w_out,
               o_norm, o_w_in, o_q_lora_gain, o_w_uq, o_kv_gain, o_w_ukv, o_q_gain, o_k_gain, o_w_out,
               f_norm, f_w_up, f_conv_w, f_conv_b, f_w_down)
    return (_trunk(x_prompt, *weights), _trunk(x_sample, *weights))
```

```python
import functools

import numpy as np
import jax
import jax.numpy as jnp
from jax import lax
from jax.experimental import pallas as pl
from jax.experimental.pallas import tpu as pltpu

D_MODEL = 1024
HEAD_DIM = 64
ROPE_THETA = 10000.0
EPS = 1e-6
NEG = -1e30
A_Q_HEADS = 8
A_KV_HEADS = 2
A_HALF_WINDOW = 128
B_HEADS = 4
B_CONFIGS = ((128, 1), (512, 4), (2048, 16))
N_BRANCH = len(B_CONFIGS)
A_Q_DIM = A_Q_HEADS * HEAD_DIM
A_KV_DIM = A_KV_HEADS * HEAD_DIM
B_DIM = B_HEADS * HEAD_DIM
C_HEADS = 16
C_NOPE = 64
C_ROPE = 32
C_QK = C_NOPE + C_ROPE
C_V = 64
C_Q_LORA = 256
C_KV_LORA = 256
D_FF = 2816
DEPTH = 2

LANES = 128
MXU_N = 256
BF16_ROWS = 16
VMEM_LIMIT = 56 * 1024 * 1024

TM = 512
ATT_TQ = 128
MLA_TQ = 512
MLA_TK = 512
FF_CHUNK = 256
N_FF_CHUNK = D_FF // FF_CHUNK
HALO = BF16_ROWS

NR_BLOCKS = 9
P0_COLS = 13 * MXU_N

_f32 = jnp.float32
_bf16 = jnp.bfloat16


def _const_spec(shape):
    nd = len(shape)
    return pl.BlockSpec(shape, lambda *_: (0,) * nd, pipeline_mode=pl.Buffered(1))


def _params(n_axes):
    return pltpu.CompilerParams(dimension_semantics=("arbitrary",) * n_axes,
                                vmem_limit_bytes=VMEM_LIMIT)


def _rms(x, g):
    y = x * lax.rsqrt(jnp.mean(x * x, axis=-1, keepdims=True) + EPS)
    return y * g


def _group_sumsq(a, gm):
    sq = a * a
    hi = sq.astype(_bf16)
    lo = (sq - hi.astype(_f32)).astype(_bf16)
    return (jnp.dot(hi, gm, preferred_element_type=_f32)
            + jnp.dot(lo, gm, preferred_element_type=_f32))


def _norm_rope(a, gm, gain, cos, sin, inv_dim):
    ss = _group_sumsq(a, gm)
    y = a * lax.rsqrt(ss * inv_dim + EPS) * gain
    halves = []
    for t in range(2):
        yt = y[:, t * LANES:(t + 1) * LANES]
        halves.append(yt * cos + pltpu.roll(yt, LANES // 2, 1) * sin)
    return halves


def _proj0_kernel(x_ref, g_ref, w_ref, hg_ref, gm_ref, cos_ref, sin_ref,
                  qa_ref, kd_ref, vd_ref,
                  bq0_ref, bk0_ref, bv0_ref, bq1_ref, bk1_ref, bv1_ref, bq2_ref, bk2_ref, bv2_ref,
                  h_scr, fold_scr):
    tm = x_ref.shape[0]
    h_scr[...] = _rms(x_ref[...], g_ref[...]).astype(_bf16)
    cos = cos_ref[...]
    sin = sin_ref[...]
    gm = gm_ref[...]

    def matmul_block(j):
        return jnp.dot(h_scr[...], w_ref[:, j * MXU_N:(j + 1) * MXU_N], preferred_element_type=_f32)

    def store_folded(out_ref, val, r):
        if r == 1:
            out_ref[...] = val.astype(_bf16)
            return
        for t in range(2):
            fold_scr[t] = val[:, t * LANES:(t + 1) * LANES]
        for c in range(r):
            for t in range(2):
                lo = c * MXU_N + t * LANES
                out_ref[:, lo:lo + LANES] = fold_scr[t, pl.ds(c, tm // r, stride=r), :].astype(_bf16)

    def nr_block(j):
        lo, hi = _norm_rope(matmul_block(j), gm, hg_ref[:, j * MXU_N:(j + 1) * MXU_N],
                            cos, sin, 1.0 / HEAD_DIM)
        return jnp.concatenate([lo, hi], axis=1)

    qa_ref[:, 0:MXU_N] = nr_block(0).astype(_bf16)
    qa_ref[:, MXU_N:2 * MXU_N] = nr_block(1).astype(_bf16)
    kd_ref[...] = nr_block(2).astype(_bf16)
    b_refs = ((bq0_ref, bk0_ref, bv0_ref), (bq1_ref, bk1_ref, bv1_ref), (bq2_ref, bk2_ref, bv2_ref))
    for g, (_, r) in enumerate(B_CONFIGS):
        store_folded(b_refs[g][0], nr_block(3 + 2 * g), r)
        store_folded(b_refs[g][1], nr_block(4 + 2 * g), r)
    vd_ref[...] = matmul_block(NR_BLOCKS).astype(_bf16)
    for g, (_, r) in enumerate(B_CONFIGS):
        store_folded(b_refs[g][2], matmul_block(NR_BLOCKS + 1 + g), r)


def _proj0(x, seq, norm_g, w, head_gain, gm, cos, sin):
    t_rows = x.shape[0]
    tm = TM
    n_pos_blocks = seq // tm
    row = lambda i: (i, 0)
    out_shapes = [jax.ShapeDtypeStruct((t_rows, A_Q_DIM), _bf16),
                  jax.ShapeDtypeStruct((t_rows, MXU_N), _bf16),
                  jax.ShapeDtypeStruct((t_rows, MXU_N), _bf16)]
    out_specs = [pl.BlockSpec((tm, A_Q_DIM), row), pl.BlockSpec((tm, MXU_N), row),
                 pl.BlockSpec((tm, MXU_N), row)]
    for _, r in B_CONFIGS:
        for _ in range(3):
            out_shapes.append(jax.ShapeDtypeStruct((t_rows // r, r * B_DIM), _bf16))
            out_specs.append(pl.BlockSpec((tm // r, r * B_DIM), row))
    return pl.pallas_call(
        _proj0_kernel,
        name="proj0",
        grid=(t_rows // tm,),
        in_specs=[pl.BlockSpec((tm, D_MODEL), row),
                  _const_spec((1, D_MODEL)),
                  _const_spec((D_MODEL, P0_COLS)),
                  _const_spec((1, NR_BLOCKS * MXU_N)),
                  _const_spec((MXU_N, MXU_N)),
                  pl.BlockSpec((tm, LANES), lambda i: (i % n_pos_blocks, 0)),
                  pl.BlockSpec((tm, LANES), lambda i: (i % n_pos_blocks, 0))],
        out_specs=out_specs,
        out_shape=out_shapes,
        scratch_shapes=[pltpu.VMEM((tm, D_MODEL), _bf16), pltpu.VMEM((2, tm, LANES), _f32)],
        compiler_params=_params(1),
    )(x, norm_g, w, head_gain, gm, cos, sin)


def _local_attn_kernel(sink_ref, q_ref, k_ref, v_ref, *out_refs, seq, tq, half, has_sink, want_lse):
    o_ref = out_refs[0]
    pair = pl.program_id(1)
    win = tq + 2 * half
    lane = lax.broadcasted_iota(jnp.int32, (1, LANES), 1)
    first_qk = ((lane // 32) % 2) == 0
    first_v = lane < HEAD_DIM
    row = lax.broadcasted_iota(jnp.int32, (2 * tq, 1), 0)
    top = row < tq
    qoff = jnp.where(top, row, row - tq)
    koff = lax.broadcasted_iota(jnp.int32, (1, win), 1)
    if has_sink:
        sink = jnp.where(top, sink_ref[pair, 0], sink_ref[pair, 1])

    def step(i, carry):
        qs = pl.multiple_of(i * tq, tq)
        ws = pl.multiple_of(jnp.clip(qs - half, 0, seq - win), half)
        q2 = q_ref[0, pl.ds(qs, tq), :]
        zero = jnp.zeros_like(q2)
        qq = jnp.concatenate([jnp.where(first_qk, q2, zero), jnp.where(first_qk, zero, q2)], axis=0)
        kw = k_ref[0, pl.ds(ws, win), :]
        vw = v_ref[0, pl.ds(ws, win), :]
        s = lax.dot_general(qq, kw, (((1,), (1,)), ((), ())), preferred_element_type=_f32)
        valid = jnp.abs((qs + qoff) - (ws + koff)) <= half
        s = jnp.where(valid, s, NEG)
        m = jnp.max(s, axis=-1, keepdims=True)
        if has_sink:
            m = jnp.maximum(m, sink)
        p = jnp.exp(s - m)
        den = jnp.sum(p, axis=-1, keepdims=True)
        if has_sink:
            den = den + jnp.exp(sink - m)
        pv = jnp.dot(p.astype(_bf16), vw, preferred_element_type=_f32)
        od = pv / den
        o_ref[0, pl.ds(qs, tq), :] = jnp.where(first_v, od[:tq], od[tq:]).astype(o_ref.dtype)
        if want_lse:
            lse = m + jnp.log(den)
            out_refs[1][0, pl.ds(qs, tq), :] = jnp.where(first_v, lse[:tq], lse[tq:])
        return carry

    lax.fori_loop(0, seq // tq, step, 0)


def _local_attn(q, k, v, sink, *, half, kv_of_pair, want_lse):
    bsz, seq, width = q.shape
    n_pairs = width // LANES
    tq = ATT_TQ
    assert seq % tq == 0 and seq >= tq + 2 * half
    has_sink = sink is not None
    if not has_sink:
        sink = jnp.zeros((1, 2), _f32)
    qmap = lambda b, p: (b, 0, p)
    kmap = lambda b, p: (b, 0, kv_of_pair(p))
    out_shape = [jax.ShapeDtypeStruct((bsz, seq, width), _bf16)]
    out_specs = [pl.BlockSpec((1, seq, LANES), qmap)]
    if want_lse:
        out_shape.append(jax.ShapeDtypeStruct((bsz, seq, width), _f32))
        out_specs.append(pl.BlockSpec((1, seq, LANES), qmap))
    return pl.pallas_call(
        functools.partial(_local_attn_kernel, seq=seq, tq=tq, half=half, has_sink=has_sink,
                          want_lse=want_lse),
        name="local_attn_h%d" % half,
        grid=(bsz, n_pairs),
        in_specs=[pl.BlockSpec(memory_space=pltpu.SMEM),
                  pl.BlockSpec((1, seq, LANES), qmap),
                  pl.BlockSpec((1, seq, LANES), kmap),
                  pl.BlockSpec((1, seq, LANES), kmap)],
        out_specs=out_specs,
        out_shape=out_shape,
        compiler_params=_params(2),
    )(sink, q, k, v)


def _out0_kernel(x_ref, a_ref, o0_ref, l0_ref, o1_ref, l1_ref, o2_ref, l2_ref, wa_ref, wb_ref,
                 y_ref, o_scr, l_scr):
    tm = x_ref.shape[0]

    def unfold(src_ref, dst_scr, r):
        if r == 1:
            return src_ref[...].astype(_f32)
        for c in range(r):
            for t in range(2):
                lo = c * MXU_N + t * LANES
                dst_scr[t, pl.ds(c, tm // r, stride=r), :] = src_ref[:, lo:lo + LANES].astype(_f32)
        return jnp.concatenate([dst_scr[0], dst_scr[1]], axis=1)

    outs, lses = [], []
    for (o_ref, l_ref), (_, r) in zip(((o0_ref, l0_ref), (o1_ref, l1_ref), (o2_ref, l2_ref)), B_CONFIGS):
        outs.append(unfold(o_ref, o_scr, r))
        lses.append(unfold(l_ref, l_scr, r))
    m = jnp.maximum(jnp.maximum(lses[0], lses[1]), lses[2])
    es = [jnp.exp(l - m) for l in lses]
    tot = es[0] + es[1] + es[2]
    b_out = (es[0] / tot) * outs[0] + (es[1] / tot) * outs[1] + (es[2] / tot) * outs[2]
    y = x_ref[...] + jnp.dot(a_ref[...], wa_ref[...], preferred_element_type=_f32)
    y_ref[...] = y + jnp.dot(b_out.astype(_bf16), wb_ref[...], preferred_element_type=_f32)


def _out0(x, a_out, b_outs, b_lses, wa, wb):
    t_rows = x.shape[0]
    tm = TM
    row = lambda i: (i, 0)
    in_specs = [pl.BlockSpec((tm, D_MODEL), row), pl.BlockSpec((tm, A_Q_DIM), row)]
    args = [x, a_out]
    for (_, r), o, l in zip(B_CONFIGS, b_outs, b_lses):
        in_specs += [pl.BlockSpec((tm // r, r * B_DIM), row)] * 2
        args += [o, l]
    in_specs += [_const_spec((A_Q_DIM, D_MODEL)), _const_spec((B_DIM, D_MODEL))]
    args += [wa, wb]
    return pl.pallas_call(
        _out0_kernel,
        name="out0",
        grid=(t_rows // tm,),
        in_specs=in_specs,
        out_specs=pl.BlockSpec((tm, D_MODEL), row),
        out_shape=jax.ShapeDtypeStruct((t_rows, D_MODEL), _f32),
        scratch_shapes=[pltpu.VMEM((2, tm, LANES), _f32), pltpu.VMEM((2, tm, LANES), _f32)],
        compiler_params=_params(1),
    )(*args)


def _ffn_kernel(xp_ref, x_ref, xn_ref, g_ref, wg_ref, wv_ref, cw_ref, wd_ref, y_ref,
                h_scr, gate_scr, acc_scr, *, tiles_per_seq):
    tm = x_ref.shape[0]
    i = pl.program_id(0)
    pos = i % tiles_per_seq
    keep_prev = jnp.where(pos == 0, 0.0, 1.0)
    keep_next = jnp.where(pos == tiles_per_seq - 1, 0.0, 1.0)
    g = g_ref[...]
    h_scr[0:HALO, :] = _rms(xp_ref[...], g).astype(_bf16)
    h_scr[HALO:HALO + tm, :] = _rms(x_ref[...], g).astype(_bf16)
    h_scr[HALO + tm:, :] = _rms(xn_ref[...], g).astype(_bf16)
    acc_scr[...] = jnp.zeros_like(acc_scr)

    def chunk(j, carry):
        gate_scr[...] = jnp.dot(h_scr[...], wg_ref[j], preferred_element_type=_f32)
        val = jnp.dot(h_scr[HALO:HALO + tm, :], wv_ref[j], preferred_element_type=_f32)
        gate_scr[HALO - 8:HALO, :] = gate_scr[HALO - 8:HALO, :] * keep_prev
        gate_scr[HALO + tm:HALO + tm + 8, :] = gate_scr[HALO + tm:HALO + tm + 8, :] * keep_next
        cw = cw_ref[j]
        conv = (cw[0:1] * gate_scr[HALO - 1:HALO - 1 + tm, :]
                + cw[1:2] * gate_scr[HALO:HALO + tm, :]
                + cw[2:3] * gate_scr[HALO + 1:HALO + 1 + tm, :]
                + cw[3:4])
        act = (jax.nn.gelu(conv) * val).astype(_bf16)
        acc_scr[...] += jnp.dot(act, wd_ref[j], preferred_element_type=_f32)
        return carry

    lax.fori_loop(0, N_FF_CHUNK, chunk, 0)
    y_ref[...] = x_ref[...] + acc_scr[...]


def _ffn(x, seq, norm_g, wg, wv, cw, wd):
    t_rows = x.shape[0]
    tm = TM
    per = tm // HALO
    n_halo_blocks = t_rows // HALO
    return pl.pallas_call(
        functools.partial(_ffn_kernel, tiles_per_seq=seq // tm),
        name="conv_ffn",
        grid=(t_rows // tm,),
        in_specs=[pl.BlockSpec((HALO, D_MODEL), lambda i: (jnp.maximum(i * per - 1, 0), 0)),
                  pl.BlockSpec((tm, D_MODEL), lambda i: (i, 0)),
                  pl.BlockSpec((HALO, D_MODEL), lambda i: (jnp.minimum((i + 1) * per, n_halo_blocks - 1), 0)),
                  _const_spec((1, D_MODEL)),
                  _const_spec((N_FF_CHUNK, D_MODEL, FF_CHUNK)),
                  _const_spec((N_FF_CHUNK, D_MODEL, FF_CHUNK)),
                  _const_spec((N_FF_CHUNK, 8, FF_CHUNK)),
                  _const_spec((N_FF_CHUNK, FF_CHUNK, D_MODEL))],
        out_specs=pl.BlockSpec((tm, D_MODEL), lambda i: (i, 0)),
        out_shape=jax.ShapeDtypeStruct((t_rows, D_MODEL), _f32),
        scratch_shapes=[pltpu.VMEM((tm + 2 * HALO, D_MODEL), _bf16),
                        pltpu.VMEM((tm + 2 * HALO, FF_CHUNK), _f32),
                        pltpu.VMEM((tm, D_MODEL), _f32)],
        compiler_params=_params(1),
    )(x, x, x, norm_g, wg, wv, cw, wd)


def _proj1_kernel(x_ref, g_ref, win_ref, qlg_ref, kvg_ref, wq_ref, wk_ref, wvt_ref, qg_ref, kg_ref,
                  gm_ref, cos_ref, sin_ref, q_ref, k_ref, vt_ref):
    tm = x_ref.shape[0]
    h = _rms(x_ref[...], g_ref[...]).astype(_bf16)
    proj = jnp.dot(h, win_ref[...], preferred_element_type=_f32)
    cq = _rms(proj[:, :C_Q_LORA], qlg_ref[...]).astype(_bf16)
    ckv_f32 = _rms(proj[:, C_Q_LORA:C_Q_LORA + C_KV_LORA], kvg_ref[...])
    ckv = ckv_f32.astype(_bf16)
    ckv_t = ckv_f32.T.astype(_bf16)
    k_rope = proj[:, C_Q_LORA + C_KV_LORA:]
    k_rope2 = jnp.concatenate([k_rope, k_rope], axis=1)
    cos = cos_ref[...]
    sin = sin_ref[...]
    gm = gm_ref[...]
    qg = jnp.concatenate([qg_ref[...]] * 2, axis=1)
    kg = jnp.concatenate([kg_ref[...]] * 2, axis=1)
    for b in range(C_HEADS // 2):
        cols = slice(b * MXU_N, (b + 1) * MXU_N)
        aq = jnp.dot(cq, wq_ref[:, cols], preferred_element_type=_f32)
        lo, hi = _norm_rope(aq, gm, qg, cos, sin, 1.0 / C_QK)
        q_ref[:, cols] = jnp.concatenate([lo, hi], axis=1).astype(_bf16)
        ak = jnp.dot(ckv, wk_ref[:, cols], preferred_element_type=_f32) + k_rope2
        lo, hi = _norm_rope(ak, gm, kg, cos, sin, 1.0 / C_QK)
        k_ref[:, cols] = jnp.concatenate([lo, hi], axis=1).astype(_bf16)
    r = lax.broadcasted_iota(jnp.int32, (MXU_N, tm), 0)
    ones_rows = (r >= C_V) & (r < MXU_N - C_V)
    for b in range(C_HEADS // 2):
        rows = slice(b * MXU_N, (b + 1) * MXU_N)
        vt = jnp.dot(wvt_ref[rows, :], ckv_t, preferred_element_type=_f32)
        vt_ref[0, rows, :] = jnp.where(ones_rows, 1.0, vt).astype(_bf16)


def _proj1(x, seq, norm_g, win, qlg, kvg, wq, wk, wvt, qg, kg, gm, cos, sin):
    t_rows = x.shape[0]
    tm = TM
    n_pos_blocks = seq // tm
    row = lambda i: (i, 0)
    qk_w = C_HEADS * LANES
    v_w = C_HEADS * LANES
    return pl.pallas_call(
        _proj1_kernel,
        name="proj1",
        grid=(t_rows // tm,),
        in_specs=[pl.BlockSpec((tm, D_MODEL), row),
                  _const_spec((1, D_MODEL)),
                  _const_spec(win.shape),
                  _const_spec((1, C_Q_LORA)), _const_spec((1, C_KV_LORA)),
                  _const_spec((C_Q_LORA, qk_w)), _const_spec((C_KV_LORA, qk_w)),
                  _const_spec((v_w, C_KV_LORA)),
                  _const_spec((1, LANES)), _const_spec((1, LANES)),
                  _const_spec((MXU_N, MXU_N)),
                  pl.BlockSpec((tm, LANES), lambda i: (i % n_pos_blocks, 0)),
                  pl.BlockSpec((tm, LANES), lambda i: (i % n_pos_blocks, 0))],
        out_specs=[pl.BlockSpec((tm, qk_w), row), pl.BlockSpec((tm, qk_w), row),
                   pl.BlockSpec((1, v_w, tm), lambda i: (i, 0, 0))],
        out_shape=[jax.ShapeDtypeStruct((t_rows, qk_w), _bf16),
                   jax.ShapeDtypeStruct((t_rows, qk_w), _bf16),
                   jax.ShapeDtypeStruct((t_rows // tm, v_w, tm), _bf16)],
        compiler_params=_params(1),
    )(x, norm_g, win, qlg, kvg, wq, wk, wvt, qg, kg, gm, cos, sin)


def _mla_kernel(q_ref, k_ref, vt_ref, o_ref, sa_ref, sb_ref, *, seq, tk):
    tq = q_ref.shape[1]
    n_tiles = seq // tk
    s_bufs = (sa_ref, sb_ref)

    def scores(t, buf):
        ks = pl.multiple_of(t * tk, tk)
        tmax = []
        for hh in range(2):
            kt = k_ref[0, pl.ds(ks, tk), hh * LANES:(hh + 1) * LANES]
            qh = q_ref[0, :, hh * LANES:(hh + 1) * LANES]
            s = lax.dot_general(kt, qh, (((1,), (1,)), ((), ())), preferred_element_type=_f32)
            buf[hh] = s
            tmax.append(jnp.max(s, axis=0, keepdims=True))
        return tuple(tmax)

    def consume(t, buf, tmax, state):
        new = []
        for hh in range(2):
            m_old, acc = state[hh]
            m_new = jnp.maximum(m_old, tmax[hh])
            alpha = jnp.exp2(m_old - m_new)
            p = jnp.exp2(buf[hh] - m_new).astype(_bf16)
            vt = vt_ref[t, hh * LANES:(hh + 1) * LANES, :]
            acc = alpha * acc + jnp.dot(vt, p, preferred_element_type=_f32)
            new.append((m_new, acc))
        return tuple(new)

    def step(jj, carry):
        tmax, state = carry
        t = 2 * jj
        tmax_b = scores(t + 1, s_bufs[1])
        state = consume(t, s_bufs[0], tmax, state)
        tmax_a = scores(t + 2, s_bufs[0])
        state = consume(t + 1, s_bufs[1], tmax_b, state)
        return tmax_a, state

    state = tuple((jnp.full((1, tq), -jnp.inf, _f32), jnp.zeros((LANES, tq), _f32)) for _ in range(2))
    tmax = scores(0, s_bufs[0])
    tmax, state = lax.fori_loop(0, n_tiles // 2 - 1, step, (tmax, state))
    tmax_b = scores(n_tiles - 1, s_bufs[1])
    state = consume(n_tiles - 2, s_bufs[0], tmax, state)
    (_, acc0), (_, acc1) = consume(n_tiles - 1, s_bufs[1], tmax_b, state)
    num = jnp.concatenate([acc0[:C_V], acc1[C_V:]], axis=0)
    den = jnp.concatenate([acc0[C_V:], acc1[:C_V]], axis=0)
    o_ref[0] = (num / den).T.astype(o_ref.dtype)


def _mla_attn(q, k, vt):
    bsz, seq, _ = q.shape
    n_pairs = C_HEADS // 2
    tq, tk = MLA_TQ, MLA_TK
    assert vt.shape[2] == tk and seq % (2 * tk) == 0 and seq % tq == 0
    return pl.pallas_call(
        functools.partial(_mla_kernel, seq=seq, tk=tk),
        name="mla_attn",
        grid=(bsz, n_pairs, seq // tq),
        in_specs=[pl.BlockSpec((1, tq, 2 * LANES), lambda b, p, i: (b, i, p)),
                  pl.BlockSpec((1, seq, 2 * LANES), lambda b, p, i: (b, 0, p)),
                  pl.BlockSpec((seq // tk, 2 * LANES, tk), lambda b, p, i: (b, p, 0))],
        out_specs=pl.BlockSpec((1, tq, LANES), lambda b, p, i: (b, i, p)),
        out_shape=jax.ShapeDtypeStruct((bsz, seq, C_HEADS * C_V), _bf16),
        scratch_shapes=[pltpu.VMEM((2, tk, tq), _f32), pltpu.VMEM((2, tk, tq), _f32)],
        compiler_params=_params(3),
    )(q, k, vt)


def _out1_kernel(x_ref, o_ref, w_ref, y_ref):
    y_ref[...] = x_ref[...] + jnp.dot(o_ref[...], w_ref[...], preferred_element_type=_f32)


def _out1(x, o, w):
    t_rows = x.shape[0]
    tm = TM
    row = lambda i: (i, 0)
    return pl.pallas_call(
        _out1_kernel,
        name="out1",
        grid=(t_rows // tm,),
        in_specs=[pl.BlockSpec((tm, D_MODEL), row), pl.BlockSpec((tm, o.shape[1]), row),
                  _const_spec(w.shape)],
        out_specs=pl.BlockSpec((tm, D_MODEL), row),
        out_shape=jax.ShapeDtypeStruct((t_rows, D_MODEL), _f32),
        compiler_params=_params(1),
    )(x, o, w)


def _pair_cols(a, b):
    r = np.arange(HEAD_DIM // 2)
    return np.concatenate([a + r, b + r, a + HEAD_DIM // 2 + r, b + HEAD_DIM // 2 + r])


_PAIR_DIMS = _pair_cols(0, 0)


def _layer0_columns():
    nr, val = [], []
    for p in range(A_Q_HEADS // 2):
        nr.append(_pair_cols(2 * p * HEAD_DIM, (2 * p + 1) * HEAD_DIM))
    for h in range(A_KV_HEADS):
        nr.append(_pair_cols(A_Q_DIM + h * HEAD_DIM, A_Q_DIM + h * HEAD_DIM))
    b0 = A_Q_DIM + 2 * A_KV_DIM
    for g in range(N_BRANCH):
        for t in range(2):
            base = b0 + g * 3 * B_DIM + t * B_DIM
            for p in range(B_HEADS // 2):
                nr.append(_pair_cols(base + 2 * p * HEAD_DIM, base + (2 * p + 1) * HEAD_DIM))
    d = np.arange(HEAD_DIM)
    for h in range(A_KV_HEADS):
        base = A_Q_DIM + A_KV_DIM + h * HEAD_DIM
        val.append(np.concatenate([base + d, base + d]))
    for g in range(N_BRANCH):
        val.append(b0 + g * 3 * B_DIM + 2 * B_DIM + np.arange(B_DIM))
    return np.concatenate(nr + val)


def _layer0_gains(a_q_gain, a_k_gain, b_q_gain, b_k_gain):
    scale = HEAD_DIM ** -0.5
    aq = a_q_gain[_PAIR_DIMS] * scale
    ak = a_k_gain[_PAIR_DIMS]
    parts = [aq] * (A_Q_HEADS // 2) + [ak] * A_KV_HEADS
    for g in range(N_BRANCH):
        parts += [b_q_gain[g][_PAIR_DIMS] * scale] * (B_HEADS // 2)
        parts += [b_k_gain[g][_PAIR_DIMS]] * (B_HEADS // 2)
    return jnp.concatenate(parts)[None, :]


def _layer0_group_matrix():
    lane = np.arange(MXU_N)
    head = (lane // LANES) * 2 + (lane // 32) % 2
    return jnp.asarray(head[:, None] == head[None, :], _bf16)


def _rope_tables(seq, dim):
    inv = jnp.power(ROPE_THETA, -jnp.arange(0, dim, 2, dtype=_f32) / dim)
    ang = jnp.arange(seq, dtype=_f32)[:, None] * inv[None, :]
    return jnp.cos(ang), jnp.sin(ang)


def _layer0_rope(seq):
    cos, sin = _rope_tables(seq, HEAD_DIM)
    return (jnp.concatenate([cos, cos, cos, cos], axis=1),
            jnp.concatenate([-sin, -sin, sin, sin], axis=1))


_C_HALF = C_ROPE // 2
_C_LANE_DIM = np.full(LANES, -1)
_C_LANE_DIM[0:_C_HALF] = C_NOPE + np.arange(_C_HALF)
_C_LANE_DIM[_C_HALF:_C_HALF + 32] = np.arange(32)
_C_LANE_DIM[64:64 + _C_HALF] = C_NOPE + _C_HALF + np.arange(_C_HALF)
_C_LANE_DIM[64 + _C_HALF:64 + _C_HALF + 32] = 32 + np.arange(32)


def _place_heads(w, per_head, lane_dim):
    idx = (np.arange(C_HEADS)[:, None] * per_head + np.maximum(lane_dim, 0)[None, :]).reshape(-1)
    keep = np.tile(lane_dim >= 0, C_HEADS)
    return jnp.where(jnp.asarray(keep)[None, :], w[:, idx], 0.0)


def _layer1_rope(seq):
    cos, sin = _rope_tables(seq, C_ROPE)
    ones = jnp.ones((seq, 64 - _C_HALF), _f32)
    zeros = jnp.zeros((seq, 64 - _C_HALF), _f32)
    return (jnp.concatenate([cos, ones, cos, ones], axis=1),
            jnp.concatenate([-sin, zeros, sin, zeros], axis=1))


def _layer1_group_matrix():
    lane = np.arange(MXU_N)
    return jnp.asarray((lane[:, None] // LANES) == (lane[None, :] // LANES), _bf16)


def _trunk(x3, e_norm, e_w_in, e_a_q_gain, e_a_k_gain, e_a_sink, e_b_q_gain, e_b_k_gain, e_w_out,
           o_norm, o_w_in, o_q_lora_gain, o_w_uq, o_kv_gain, o_w_ukv, o_q_gain, o_k_gain, o_w_out,
           f_norm, f_w_up, f_conv_w, f_conv_b, f_w_down):
    bsz, seq, _ = x3.shape
    assert seq % TM == 0
    x = x3.reshape(bsz * seq, D_MODEL)
    for layer in range(DEPTH):
        i = layer // 2
        if layer % 2 == 0:
            w0 = e_w_in[i][:, _layer0_columns()].astype(_bf16)
            cos, sin = _layer0_rope(seq)
            outs = _proj0(x, seq, e_norm[i][None, :], w0,
                          _layer0_gains(e_a_q_gain[i], e_a_k_gain[i], e_b_q_gain[i], e_b_k_gain[i]),
                          _layer0_group_matrix(), cos, sin)
            qa, kd, vd = outs[:3]
            a_out = _local_attn(qa.reshape(bsz, seq, A_Q_DIM), kd.reshape(bsz, seq, MXU_N),
                                vd.reshape(bsz, seq, MXU_N), e_a_sink[i].reshape(A_Q_HEADS // 2, 2),
                                half=A_HALF_WINDOW, kv_of_pair=lambda p: p // 2, want_lse=False)[0]
            b_outs, b_lses = [], []
            for g, (window, r) in enumerate(B_CONFIGS):
                bq, bk, bv = (t.reshape(bsz, seq // r, r * B_DIM) for t in outs[3 + 3 * g:6 + 3 * g])
                o, lse = _local_attn(bq, bk, bv, None, half=(window // 2) // r,
                                     kv_of_pair=lambda p: p, want_lse=True)
                b_outs.append(o.reshape(bsz * seq // r, r * B_DIM))
                b_lses.append(lse.reshape(bsz * seq // r, r * B_DIM))
            w_out = e_w_out[i].astype(_bf16)
            x = _out0(x, a_out.reshape(bsz * seq, A_Q_DIM), b_outs, b_lses,
                      w_out[:A_Q_DIM], w_out[A_Q_DIM:])
        else:
            w_in = o_w_in[i]
            k_rope_cols = jnp.zeros((D_MODEL, LANES), _f32)
            k_rope_cols = k_rope_cols.at[:, 0:_C_HALF].set(w_in[:, C_Q_LORA + C_KV_LORA:C_Q_LORA + C_KV_LORA + _C_HALF])
            k_rope_cols = k_rope_cols.at[:, 64:64 + _C_HALF].set(w_in[:, C_Q_LORA + C_KV_LORA + _C_HALF:])
            win = jnp.concatenate([w_in[:, :C_Q_LORA + C_KV_LORA], k_rope_cols], axis=1).astype(_bf16)
            wq = _place_heads(o_w_uq[i], C_QK, _C_LANE_DIM).astype(_bf16)
            nope_only = np.where(_C_LANE_DIM < C_NOPE, _C_LANE_DIM, -1)
            wk = _place_heads(o_w_ukv[i], C_NOPE + C_V, nope_only).astype(_bf16)
            r_in_pair = np.arange(C_HEADS * LANES) % MXU_N
            head = np.arange(C_HEADS * LANES) // LANES
            is_value = (r_in_pair < C_V) | (r_in_pair >= MXU_N - C_V)
            v_col = head * (C_NOPE + C_V) + C_NOPE + r_in_pair % C_V
            wvt = jnp.where(jnp.asarray(is_value)[:, None], o_w_ukv[i][:, v_col].T, 0.0).astype(_bf16)
            lane_ok = jnp.asarray(_C_LANE_DIM >= 0)
            q_scale = (C_QK ** -0.5) * np.log2(np.e)
            qg = jnp.where(lane_ok, o_q_gain[i][np.maximum(_C_LANE_DIM, 0)] * q_scale, 0.0)[None, :]
            kg = jnp.where(lane_ok, o_k_gain[i][np.maximum(_C_LANE_DIM, 0)], 0.0)[None, :]
            cos, sin = _layer1_rope(seq)
            q, k, vt = _proj1(x, seq, o_norm[i][None, :], win, o_q_lora_gain[i][None, :],
                              o_kv_gain[i][None, :], wq, wk, wvt, qg, kg, _layer1_group_matrix(),
                              cos, sin)
            o = _mla_attn(q.reshape(bsz, seq, -1), k.reshape(bsz, seq, -1), vt)
            x = _out1(x, o.reshape(bsz * seq, C_HEADS * C_V), o_w_out[i].astype(_bf16))
        w_up = f_w_up[layer]
        wg = w_up[:, :D_FF].reshape(D_MODEL, N_FF_CHUNK, FF_CHUNK).transpose(1, 0, 2).astype(_bf16)
        wv_ = w_up[:, D_FF:].reshape(D_MODEL, N_FF_CHUNK, FF_CHUNK).transpose(1, 0, 2).astype(_bf16)
        cw = jnp.concatenate([f_conv_w[layer], f_conv_b[layer][None, :], jnp.zeros((4, D_FF), _f32)], axis=0)
        cw = cw.reshape(8, N_FF_CHUNK, FF_CHUNK).transpose(1, 0, 2)
        wd = f_w_down[layer].reshape(N_FF_CHUNK, FF_CHUNK, D_MODEL).astype(_bf16)
        x = _ffn(x, seq, f_norm[layer][None, :], wg, wv_, cw, wd)
    return x.reshape(bsz, seq, D_MODEL)


def kernel(x_prompt, x_sample, e_norm, e_w_in, e_a_q_gain, e_a_k_gain, e_a_sink, e_b_q_gain, e_b_k_gain, e_w_out, o_norm, o_w_in, o_q_lora_gain, o_w_uq, o_kv_gain, o_w_ukv, o_q_gain, o_k_gain, o_w_out, f_norm, f_w_up, f_conv_w, f_conv_b, f_w_down):
    weights = (e_norm, e_w_in, e_a_q_gain, e_a_k_gain, e_a_sink, e_b_q_gain, e_b_k_gain, e_w_out,
               o_norm, o_w_in, o_q_lora_gain, o_w_uq, o_kv_gain, o_w_ukv, o_q_gain, o_k_gain, o_w_out,
               f_norm, f_w_up, f_conv_w, f_conv_b, f_w_down)
    return (_trunk(x_prompt, *weights), _trunk(x_sample, *weights))
```

```python
import functools

import numpy as np
import jax
import jax.numpy as jnp
from jax import lax
from jax.experimental import pallas as pl
from jax.experimental.pallas import tpu as pltpu

D_MODEL = 1024
HEAD_DIM = 64
ROPE_THETA = 10000.0
EPS = 1e-6
NEG = -1e30
A_Q_HEADS = 8
A_KV_HEADS = 2
A_HALF_WINDOW = 128
B_HEADS = 4
B_CONFIGS = ((128, 1), (512, 4), (2048, 16))
N_BRANCH = len(B_CONFIGS)
A_Q_DIM = A_Q_HEADS * HEAD_DIM
A_KV_DIM = A_KV_HEADS * HEAD_DIM
B_DIM = B_HEADS * HEAD_DIM
C_HEADS = 16
C_NOPE = 64
C_ROPE = 32
C_QK = C_NOPE + C_ROPE
C_V = 64
C_Q_LORA = 256
C_KV_LORA = 256
D_FF = 2816
DEPTH = 2

LANES = 128
MXU_N = 256
BF16_ROWS = 16
VMEM_LIMIT = 56 * 1024 * 1024

TM = 512
ATT_TQ = 128
ATT_UNROLL = 4
MLA_TQ = 512
MLA_TK = 512
MLA_UNROLL = 4
FF_CHUNK = 256
N_FF_CHUNK = D_FF // FF_CHUNK
HALO = BF16_ROWS

NR_BLOCKS = 9
P0_COLS = 13 * MXU_N

_f32 = jnp.float32
_bf16 = jnp.bfloat16


def _const_spec(shape):
    nd = len(shape)
    return pl.BlockSpec(shape, lambda *_: (0,) * nd, pipeline_mode=pl.Buffered(1))


def _params(n_axes):
    return pltpu.CompilerParams(dimension_semantics=("arbitrary",) * n_axes,
                                vmem_limit_bytes=VMEM_LIMIT)


def _rms(x, g):
    y = x * lax.rsqrt(jnp.mean(x * x, axis=-1, keepdims=True) + EPS)
    return y * g


def _group_sumsq(a, gm):
    sq = a * a
    hi = sq.astype(_bf16)
    lo = (sq - hi.astype(_f32)).astype(_bf16)
    return (jnp.dot(hi, gm, preferred_element_type=_f32)
            + jnp.dot(lo, gm, preferred_element_type=_f32))


def _norm_rope(a, gm, gain, cos, sin, inv_dim):
    ss = _group_sumsq(a, gm)
    y = a * lax.rsqrt(ss * inv_dim + EPS) * gain
    halves = []
    for t in range(2):
        yt = y[:, t * LANES:(t + 1) * LANES]
        halves.append(yt * cos + pltpu.roll(yt, LANES // 2, 1) * sin)
    return halves


def _proj0_kernel(x_ref, g_ref, w_ref, hg_ref, gm_ref, cos_ref, sin_ref,
                  qa_ref, kd_ref, vd_ref,
                  bq0_ref, bk0_ref, bv0_ref, bq1_ref, bk1_ref, bv1_ref, bq2_ref, bk2_ref, bv2_ref,
                  h_scr, fold_scr):
    tm = x_ref.shape[0]
    h_scr[...] = _rms(x_ref[...], g_ref[...]).astype(_bf16)
    cos = cos_ref[...]
    sin = sin_ref[...]
    gm = gm_ref[...]

    blocks = {}

    def matmul_block(j):
        if j not in blocks:
            n_blk = min(2, P0_COLS // MXU_N - j)
            a = jnp.dot(h_scr[...], w_ref[:, j * MXU_N:(j + n_blk) * MXU_N], preferred_element_type=_f32)
            for t in range(n_blk):
                blocks[j + t] = a[:, t * MXU_N:(t + 1) * MXU_N]
        return blocks.pop(j)

    def store_folded(out_ref, val, r):
        if r == 1:
            out_ref[...] = val.astype(_bf16)
            return
        for t in range(2):
            fold_scr[t] = val[:, t * LANES:(t + 1) * LANES]
        for c in range(r):
            for t in range(2):
                lo = c * MXU_N + t * LANES
                out_ref[:, lo:lo + LANES] = fold_scr[t, pl.ds(c, tm // r, stride=r), :].astype(_bf16)

    def nr_block(j):
        lo, hi = _norm_rope(matmul_block(j), gm, hg_ref[:, j * MXU_N:(j + 1) * MXU_N],
                            cos, sin, 1.0 / HEAD_DIM)
        return jnp.concatenate([lo, hi], axis=1)

    qa_ref[:, 0:MXU_N] = nr_block(0).astype(_bf16)
    qa_ref[:, MXU_N:2 * MXU_N] = nr_block(1).astype(_bf16)
    kd_ref[...] = nr_block(2).astype(_bf16)
    b_refs = ((bq0_ref, bk0_ref, bv0_ref), (bq1_ref, bk1_ref, bv1_ref), (bq2_ref, bk2_ref, bv2_ref))
    for g, (_, r) in enumerate(B_CONFIGS):
        store_folded(b_refs[g][0], nr_block(3 + 2 * g), r)
        store_folded(b_refs[g][1], nr_block(4 + 2 * g), r)
    vd_ref[...] = matmul_block(NR_BLOCKS).astype(_bf16)
    for g, (_, r) in enumerate(B_CONFIGS):
        store_folded(b_refs[g][2], matmul_block(NR_BLOCKS + 1 + g), r)


def _proj0(x, seq, norm_g, w, head_gain, gm, cos, sin):
    t_rows = x.shape[0]
    tm = TM
    n_pos_blocks = seq // tm
    row = lambda i: (i, 0)
    out_shapes = [jax.ShapeDtypeStruct((t_rows, A_Q_DIM), _bf16),
                  jax.ShapeDtypeStruct((t_rows, MXU_N), _bf16),
                  jax.ShapeDtypeStruct((t_rows, MXU_N), _bf16)]
    out_specs = [pl.BlockSpec((tm, A_Q_DIM), row), pl.BlockSpec((tm, MXU_N), row),
                 pl.BlockSpec((tm, MXU_N), row)]
    for _, r in B_CONFIGS:
        for _ in range(3):
            out_shapes.append(jax.ShapeDtypeStruct((t_rows // r, r * B_DIM), _bf16))
            out_specs.append(pl.BlockSpec((tm // r, r * B_DIM), row))
    return pl.pallas_call(
        _proj0_kernel,
        name="proj0",
        grid=(t_rows // tm,),
        in_specs=[pl.BlockSpec((tm, D_MODEL), row),
                  _const_spec((1, D_MODEL)),
                  _const_spec((D_MODEL, P0_COLS)),
                  _const_spec((1, NR_BLOCKS * MXU_N)),
                  _const_spec((MXU_N, MXU_N)),
                  pl.BlockSpec((tm, LANES), lambda i: (i % n_pos_blocks, 0)),
                  pl.BlockSpec((tm, LANES), lambda i: (i % n_pos_blocks, 0))],
        out_specs=out_specs,
        out_shape=out_shapes,
        scratch_shapes=[pltpu.VMEM((tm, D_MODEL), _bf16), pltpu.VMEM((2, tm, LANES), _f32)],
        compiler_params=_params(1),
    )(x, norm_g, w, head_gain, gm, cos, sin)


def _local_attn_kernel(sink_ref, q_ref, k_ref, v_ref, *out_refs, seq, tq, half, has_sink, want_lse):
    o_ref = out_refs[0]
    pair = pl.program_id(1)
    win = tq + 2 * half
    lane = lax.broadcasted_iota(jnp.int32, (1, LANES), 1)
    first_qk = ((lane // 32) % 2) == 0
    first_v = lane < HEAD_DIM
    row = lax.broadcasted_iota(jnp.int32, (2 * tq, 1), 0)
    top = row < tq
    qoff = jnp.where(top, row, row - tq)
    koff = lax.broadcasted_iota(jnp.int32, (1, win), 1)
    if has_sink:
        sink = jnp.where(top, sink_ref[pair, 0], sink_ref[pair, 1])

    def step(i, carry):
        qs = pl.multiple_of(i * tq, tq)
        ws = pl.multiple_of(jnp.clip(qs - half, 0, seq - win), half)
        q2 = q_ref[0, pl.ds(qs, tq), :]
        zero = jnp.zeros_like(q2)
        qq = jnp.concatenate([jnp.where(first_qk, q2, zero), jnp.where(first_qk, zero, q2)], axis=0)
        kw = k_ref[0, pl.ds(ws, win), :]
        vw = v_ref[0, pl.ds(ws, win), :]
        s = lax.dot_general(qq, kw, (((1,), (1,)), ((), ())), preferred_element_type=_f32)
        valid = jnp.abs((qs + qoff) - (ws + koff)) <= half
        s = jnp.where(valid, s, NEG)
        m = jnp.max(s, axis=-1, keepdims=True)
        if has_sink:
            m = jnp.maximum(m, sink)
        p = jnp.exp(s - m)
        den = jnp.sum(p, axis=-1, keepdims=True)
        if has_sink:
            den = den + jnp.exp(sink - m)
        pv = jnp.dot(p.astype(_bf16), vw, preferred_element_type=_f32)
        od = pv / den
        o_ref[0, pl.ds(qs, tq), :] = jnp.where(first_v, od[:tq], od[tq:]).astype(o_ref.dtype)
        if want_lse:
            lse = m + jnp.log(den)
            out_refs[1][0, pl.ds(qs, tq), :] = jnp.where(first_v, lse[:tq], lse[tq:])
        return carry

    lax.fori_loop(0, seq // tq, step, 0, unroll=min(ATT_UNROLL, seq // tq))


def _local_attn(q, k, v, sink, *, half, kv_of_pair, want_lse):
    bsz, seq, width = q.shape
    n_pairs = width // LANES
    tq = ATT_TQ
    assert seq % tq == 0 and seq >= tq + 2 * half
    has_sink = sink is not None
    if not has_sink:
        sink = jnp.zeros((1, 2), _f32)
    qmap = lambda b, p: (b, 0, p)
    kmap = lambda b, p: (b, 0, kv_of_pair(p))
    out_shape = [jax.ShapeDtypeStruct((bsz, seq, width), _bf16)]
    out_specs = [pl.BlockSpec((1, seq, LANES), qmap)]
    if want_lse:
        out_shape.append(jax.ShapeDtypeStruct((bsz, seq, width), _f32))
        out_specs.append(pl.BlockSpec((1, seq, LANES), qmap))
    return pl.pallas_call(
        functools.partial(_local_attn_kernel, seq=seq, tq=tq, half=half, has_sink=has_sink,
                          want_lse=want_lse),
        name="local_attn_h%d" % half,
        grid=(bsz, n_pairs),
        in_specs=[pl.BlockSpec(memory_space=pltpu.SMEM),
                  pl.BlockSpec((1, seq, LANES), qmap),
                  pl.BlockSpec((1, seq, LANES), kmap),
                  pl.BlockSpec((1, seq, LANES), kmap)],
        out_specs=out_specs,
        out_shape=out_shape,
        compiler_params=_params(2),
    )(sink, q, k, v)


def _out0_kernel(x_ref, a_ref, o0_ref, l0_ref, o1_ref, l1_ref, o2_ref, l2_ref, wa_ref, wb_ref,
                 y_ref, o_scr, l_scr):
    tm = x_ref.shape[0]

    def unfold(src_ref, dst_scr, r):
        if r == 1:
            return src_ref[...].astype(_f32)
        for c in range(r):
            for t in range(2):
                lo = c * MXU_N + t * LANES
                dst_scr[t, pl.ds(c, tm // r, stride=r), :] = src_ref[:, lo:lo + LANES].astype(_f32)
        return jnp.concatenate([dst_scr[0], dst_scr[1]], axis=1)

    outs, lses = [], []
    for (o_ref, l_ref), (_, r) in zip(((o0_ref, l0_ref), (o1_ref, l1_ref), (o2_ref, l2_ref)), B_CONFIGS):
        outs.append(unfold(o_ref, o_scr, r))
        lses.append(unfold(l_ref, l_scr, r))
    m = jnp.maximum(jnp.maximum(lses[0], lses[1]), lses[2])
    es = [jnp.exp(l - m) for l in lses]
    tot = es[0] + es[1] + es[2]
    b_out = (es[0] / tot) * outs[0] + (es[1] / tot) * outs[1] + (es[2] / tot) * outs[2]
    y = x_ref[...] + jnp.dot(a_ref[...], wa_ref[...], preferred_element_type=_f32)
    y_ref[...] = y + jnp.dot(b_out.astype(_bf16), wb_ref[...], preferred_element_type=_f32)


def _out0(x, a_out, b_outs, b_lses, wa, wb):
    t_rows = x.shape[0]
    tm = TM
    row = lambda i: (i, 0)
    in_specs = [pl.BlockSpec((tm, D_MODEL), row), pl.BlockSpec((tm, A_Q_DIM), row)]
    args = [x, a_out]
    for (_, r), o, l in zip(B_CONFIGS, b_outs, b_lses):
        in_specs += [pl.BlockSpec((tm // r, r * B_DIM), row)] * 2
        args += [o, l]
    in_specs += [_const_spec((A_Q_DIM, D_MODEL)), _const_spec((B_DIM, D_MODEL))]
    args += [wa, wb]
    return pl.pallas_call(
        _out0_kernel,
        name="out0",
        grid=(t_rows // tm,),
        in_specs=in_specs,
        out_specs=pl.BlockSpec((tm, D_MODEL), row),
        out_shape=jax.ShapeDtypeStruct((t_rows, D_MODEL), _f32),
        scratch_shapes=[pltpu.VMEM((2, tm, LANES), _f32), pltpu.VMEM((2, tm, LANES), _f32)],
        compiler_params=_params(1),
    )(*args)


def _ffn_kernel(xp_ref, x_ref, xn_ref, g_ref, wg_ref, wv_ref, cw_ref, wd_ref, y_ref,
                h_scr, gate_a, gate_b, val_a, val_b, acc_scr, *, tiles_per_seq):
    tm = x_ref.shape[0]
    i = pl.program_id(0)
    pos = i % tiles_per_seq
    keep_prev = jnp.where(pos == 0, 0.0, 1.0)
    keep_next = jnp.where(pos == tiles_per_seq - 1, 0.0, 1.0)
    g = g_ref[...]
    h_scr[0:HALO, :] = _rms(xp_ref[...], g).astype(_bf16)
    h_scr[HALO:HALO + tm, :] = _rms(x_ref[...], g).astype(_bf16)
    h_scr[HALO + tm:, :] = _rms(xn_ref[...], g).astype(_bf16)

    def up(j, gate_scr, val_scr):
        gate_scr[...] = jnp.dot(h_scr[...], wg_ref[j], preferred_element_type=_f32)
        gate_scr[HALO - 8:HALO, :] = gate_scr[HALO - 8:HALO, :] * keep_prev
        gate_scr[HALO + tm:HALO + tm + 8, :] = gate_scr[HALO + tm:HALO + tm + 8, :] * keep_next
        val_scr[...] = jnp.dot(h_scr[HALO:HALO + tm, :], wv_ref[j], preferred_element_type=_f32)

    def act(j, gate_scr, val_scr):
        cw = cw_ref[j]
        conv = (cw[0:1] * gate_scr[HALO - 1:HALO - 1 + tm, :]
                + cw[1:2] * gate_scr[HALO:HALO + tm, :]
                + cw[2:3] * gate_scr[HALO + 1:HALO + 1 + tm, :]
                + cw[3:4])
        return (jax.nn.gelu(conv) * val_scr[...]).astype(_bf16)

    def step(jj, carry):
        j = 2 * jj
        up(j + 1, gate_b, val_b)
        a0 = act(j, gate_a, val_a)
        up(j + 2, gate_a, val_a)
        a1 = act(j + 1, gate_b, val_b)
        acc_scr[...] += (jnp.dot(a0, wd_ref[j], preferred_element_type=_f32)
                         + jnp.dot(a1, wd_ref[j + 1], preferred_element_type=_f32))
        return carry

    acc_scr[...] = jnp.zeros_like(acc_scr)
    up(0, gate_a, val_a)
    lax.fori_loop(0, (N_FF_CHUNK - 1) // 2, step, 0)
    a_last = act(N_FF_CHUNK - 1, gate_a, val_a)
    y_ref[...] = x_ref[...] + (acc_scr[...]
                               + jnp.dot(a_last, wd_ref[N_FF_CHUNK - 1], preferred_element_type=_f32))


def _ffn(x, seq, norm_g, wg, wv, cw, wd):
    t_rows = x.shape[0]
    tm = TM
    per = tm // HALO
    n_halo_blocks = t_rows // HALO
    return pl.pallas_call(
        functools.partial(_ffn_kernel, tiles_per_seq=seq // tm),
        name="conv_ffn",
        grid=(t_rows // tm,),
        in_specs=[pl.BlockSpec((HALO, D_MODEL), lambda i: (jnp.maximum(i * per - 1, 0), 0)),
                  pl.BlockSpec((tm, D_MODEL), lambda i: (i, 0)),
                  pl.BlockSpec((HALO, D_MODEL), lambda i: (jnp.minimum((i + 1) * per, n_halo_blocks - 1), 0)),
                  _const_spec((1, D_MODEL)),
                  _const_spec((N_FF_CHUNK, D_MODEL, FF_CHUNK)),
                  _const_spec((N_FF_CHUNK, D_MODEL, FF_CHUNK)),
                  _const_spec((N_FF_CHUNK, 8, FF_CHUNK)),
                  _const_spec((N_FF_CHUNK, FF_CHUNK, D_MODEL))],
        out_specs=pl.BlockSpec((tm, D_MODEL), lambda i: (i, 0)),
        out_shape=jax.ShapeDtypeStruct((t_rows, D_MODEL), _f32),
        scratch_shapes=[pltpu.VMEM((tm + 2 * HALO, D_MODEL), _bf16),
                        pltpu.VMEM((tm + 2 * HALO, FF_CHUNK), _f32),
                        pltpu.VMEM((tm + 2 * HALO, FF_CHUNK), _f32),
                        pltpu.VMEM((tm, FF_CHUNK), _f32),
                        pltpu.VMEM((tm, FF_CHUNK), _f32),
                        pltpu.VMEM((tm, D_MODEL), _f32)],
        compiler_params=_params(1),
    )(x, x, x, norm_g, wg, wv, cw, wd)


def _proj1_kernel(x_ref, g_ref, win_ref, qlg_ref, kvg_ref, wq_ref, wk_ref, wvt_ref, qg_ref, kg_ref,
                  gm_ref, cos_ref, sin_ref, q_ref, k_ref, vt_ref):
    tm = x_ref.shape[0]
    h = _rms(x_ref[...], g_ref[...]).astype(_bf16)
    proj = jnp.dot(h, win_ref[...], preferred_element_type=_f32)
    cq = _rms(proj[:, :C_Q_LORA], qlg_ref[...]).astype(_bf16)
    ckv_f32 = _rms(proj[:, C_Q_LORA:C_Q_LORA + C_KV_LORA], kvg_ref[...])
    ckv = ckv_f32.astype(_bf16)
    ckv_t = ckv_f32.T.astype(_bf16)
    k_rope = proj[:, C_Q_LORA + C_KV_LORA:]
    k_rope2 = jnp.concatenate([k_rope, k_rope], axis=1)
    cos = cos_ref[...]
    sin = sin_ref[...]
    gm = gm_ref[...]
    qg = jnp.concatenate([qg_ref[...]] * 2, axis=1)
    kg = jnp.concatenate([kg_ref[...]] * 2, axis=1)
    for b in range(C_HEADS // 2):
        cols = slice(b * MXU_N, (b + 1) * MXU_N)
        aq = jnp.dot(cq, wq_ref[:, cols], preferred_element_type=_f32)
        lo, hi = _norm_rope(aq, gm, qg, cos, sin, 1.0 / C_QK)
        q_ref[:, cols] = jnp.concatenate([lo, hi], axis=1).astype(_bf16)
        ak = jnp.dot(ckv, wk_ref[:, cols], preferred_element_type=_f32) + k_rope2
        lo, hi = _norm_rope(ak, gm, kg, cos, sin, 1.0 / C_QK)
        k_ref[:, cols] = jnp.concatenate([lo, hi], axis=1).astype(_bf16)
    r = lax.broadcasted_iota(jnp.int32, (MXU_N, tm), 0)
    ones_rows = (r >= C_V) & (r < MXU_N - C_V)
    for b in range(C_HEADS // 2):
        rows = slice(b * MXU_N, (b + 1) * MXU_N)
        vt = jnp.dot(wvt_ref[rows, :], ckv_t, preferred_element_type=_f32)
        vt_ref[0, rows, :] = jnp.where(ones_rows, 1.0, vt).astype(_bf16)


def _proj1(x, seq, norm_g, win, qlg, kvg, wq, wk, wvt, qg, kg, gm, cos, sin):
    t_rows = x.shape[0]
    tm = TM
    n_pos_blocks = seq // tm
    row = lambda i: (i, 0)
    qk_w = C_HEADS * LANES
    v_w = C_HEADS * LANES
    return pl.pallas_call(
        _proj1_kernel,
        name="proj1",
        grid=(t_rows // tm,),
        in_specs=[pl.BlockSpec((tm, D_MODEL), row),
                  _const_spec((1, D_MODEL)),
                  _const_spec(win.shape),
                  _const_spec((1, C_Q_LORA)), _const_spec((1, C_KV_LORA)),
                  _const_spec((C_Q_LORA, qk_w)), _const_spec((C_KV_LORA, qk_w)),
                  _const_spec((v_w, C_KV_LORA)),
                  _const_spec((1, LANES)), _const_spec((1, LANES)),
                  _const_spec((MXU_N, MXU_N)),
                  pl.BlockSpec((tm, LANES), lambda i: (i % n_pos_blocks, 0)),
                  pl.BlockSpec((tm, LANES), lambda i: (i % n_pos_blocks, 0))],
        out_specs=[pl.BlockSpec((tm, qk_w), row), pl.BlockSpec((tm, qk_w), row),
                   pl.BlockSpec((1, v_w, tm), lambda i: (i, 0, 0))],
        out_shape=[jax.ShapeDtypeStruct((t_rows, qk_w), _bf16),
                   jax.ShapeDtypeStruct((t_rows, qk_w), _bf16),
                   jax.ShapeDtypeStruct((t_rows // tm, v_w, tm), _bf16)],
        compiler_params=_params(1),
    )(x, norm_g, win, qlg, kvg, wq, wk, wvt, qg, kg, gm, cos, sin)


def _mla_kernel(q_ref, k_ref, vt_ref, o_ref, sa_ref, sb_ref, *, seq, tk, unroll):
    tq = q_ref.shape[1]
    n_tiles = seq // tk
    s_bufs = (sa_ref, sb_ref)

    def scores(t, buf):
        ks = pl.multiple_of(t * tk, tk)
        tmax = []
        for hh in range(2):
            kt = k_ref[0, pl.ds(ks, tk), hh * LANES:(hh + 1) * LANES]
            qh = q_ref[0, :, hh * LANES:(hh + 1) * LANES]
            s = lax.dot_general(kt, qh, (((1,), (1,)), ((), ())), preferred_element_type=_f32)
            buf[hh] = s
            tmax.append(jnp.max(s, axis=0, keepdims=True))
        return tuple(tmax)

    def consume(t, buf, tmax, state):
        new = []
        for hh in range(2):
            m_old, acc = state[hh]
            m_new = jnp.maximum(m_old, tmax[hh])
            alpha = jnp.exp2(m_old - m_new)
            p = jnp.exp2(buf[hh] - m_new).astype(_bf16)
            vt = vt_ref[t, hh * LANES:(hh + 1) * LANES, :]
            acc = alpha * acc + jnp.dot(vt, p, preferred_element_type=_f32)
            new.append((m_new, acc))
        return tuple(new)

    def run_tiles(t0, tmax, state, score_following):
        for u in range(unroll):
            if u < unroll - 1 or score_following:
                tmax_next = scores(t0 + u + 1, s_bufs[(u + 1) % 2])
            state = consume(t0 + u, s_bufs[u % 2], tmax, state)
            tmax = tmax_next
        return tmax, state

    state = tuple((jnp.full((1, tq), -jnp.inf, _f32), jnp.zeros((LANES, tq), _f32)) for _ in range(2))
    tmax = scores(0, s_bufs[0])
    tmax, state = lax.fori_loop(
        0, n_tiles // unroll - 1,
        lambda jj, c: run_tiles(unroll * jj, c[0], c[1], True), (tmax, state))
    _, ((_, acc0), (_, acc1)) = run_tiles(n_tiles - unroll, tmax, state, False)
    num = jnp.concatenate([acc0[:C_V], acc1[C_V:]], axis=0)
    den = jnp.concatenate([acc0[C_V:], acc1[:C_V]], axis=0)
    o_ref[0] = (num / den).T.astype(o_ref.dtype)


def _mla_attn(q, k, vt):
    bsz, seq, _ = q.shape
    n_pairs = C_HEADS // 2
    tq, tk = MLA_TQ, MLA_TK
    n_tiles = seq // tk
    unroll = MLA_UNROLL if n_tiles >= 3 * MLA_UNROLL else 2
    assert vt.shape[2] == tk and n_tiles % unroll == 0 and seq % tq == 0
    return pl.pallas_call(
        functools.partial(_mla_kernel, seq=seq, tk=tk, unroll=unroll),
        name="mla_attn",
        grid=(bsz, n_pairs, seq // tq),
        in_specs=[pl.BlockSpec((1, tq, 2 * LANES), lambda b, p, i: (b, i, p)),
                  pl.BlockSpec((1, seq, 2 * LANES), lambda b, p, i: (b, 0, p)),
                  pl.BlockSpec((seq // tk, 2 * LANES, tk), lambda b, p, i: (b, p, 0))],
        out_specs=pl.BlockSpec((1, tq, LANES), lambda b, p, i: (b, i, p)),
        out_shape=jax.ShapeDtypeStruct((bsz, seq, C_HEADS * C_V), _bf16),
        scratch_shapes=[pltpu.VMEM((2, tk, tq), _f32), pltpu.VMEM((2, tk, tq), _f32)],
        compiler_params=_params(3),
    )(q, k, vt)


def _out1_kernel(x_ref, o_ref, w_ref, y_ref):
    y_ref[...] = x_ref[...] + jnp.dot(o_ref[...], w_ref[...], preferred_element_type=_f32)


def _out1(x, o, w):
    t_rows = x.shape[0]
    tm = TM
    row = lambda i: (i, 0)
    return pl.pallas_call(
        _out1_kernel,
        name="out1",
        grid=(t_rows // tm,),
        in_specs=[pl.BlockSpec((tm, D_MODEL), row), pl.BlockSpec((tm, o.shape[1]), row),
                  _const_spec(w.shape)],
        out_specs=pl.BlockSpec((tm, D_MODEL), row),
        out_shape=jax.ShapeDtypeStruct((t_rows, D_MODEL), _f32),
        compiler_params=_params(1),
    )(x, o, w)


def _pair_cols(a, b):
    r = np.arange(HEAD_DIM // 2)
    return np.concatenate([a + r, b + r, a + HEAD_DIM // 2 + r, b + HEAD_DIM // 2 + r])


_PAIR_DIMS = _pair_cols(0, 0)


def _layer0_columns():
    nr, val = [], []
    for p in range(A_Q_HEADS // 2):
        nr.append(_pair_cols(2 * p * HEAD_DIM, (2 * p + 1) * HEAD_DIM))
    for h in range(A_KV_HEADS):
        nr.append(_pair_cols(A_Q_DIM + h * HEAD_DIM, A_Q_DIM + h * HEAD_DIM))
    b0 = A_Q_DIM + 2 * A_KV_DIM
    for g in range(N_BRANCH):
        for t in range(2):
            base = b0 + g * 3 * B_DIM + t * B_DIM
            for p in range(B_HEADS // 2):
                nr.append(_pair_cols(base + 2 * p * HEAD_DIM, base + (2 * p + 1) * HEAD_DIM))
    d = np.arange(HEAD_DIM)
    for h in range(A_KV_HEADS):
        base = A_Q_DIM + A_KV_DIM + h * HEAD_DIM
        val.append(np.concatenate([base + d, base + d]))
    for g in range(N_BRANCH):
        val.append(b0 + g * 3 * B_DIM + 2 * B_DIM + np.arange(B_DIM))
    return np.concatenate(nr + val)


def _layer0_gains(a_q_gain, a_k_gain, b_q_gain, b_k_gain):
    scale = HEAD_DIM ** -0.5
    aq = a_q_gain[_PAIR_DIMS] * scale
    ak = a_k_gain[_PAIR_DIMS]
    parts = [aq] * (A_Q_HEADS // 2) + [ak] * A_KV_HEADS
    for g in range(N_BRANCH):
        parts += [b_q_gain[g][_PAIR_DIMS] * scale] * (B_HEADS // 2)
        parts += [b_k_gain[g][_PAIR_DIMS]] * (B_HEADS // 2)
    return jnp.concatenate(parts)[None, :]


def _layer0_group_matrix():
    lane = np.arange(MXU_N)
    head = (lane // LANES) * 2 + (lane // 32) % 2
    return jnp.asarray(head[:, None] == head[None, :], _bf16)


def _rope_tables(seq, dim):
    inv = jnp.power(ROPE_THETA, -jnp.arange(0, dim, 2, dtype=_f32) / dim)
    ang = jnp.arange(seq, dtype=_f32)[:, None] * inv[None, :]
    return jnp.cos(ang), jnp.sin(ang)


def _layer0_rope(seq):
    cos, sin = _rope_tables(seq, HEAD_DIM)
    return (jnp.concatenate([cos, cos, cos, cos], axis=1),
            jnp.concatenate([-sin, -sin, sin, sin], axis=1))


_C_HALF = C_ROPE // 2
_C_LANE_DIM = np.full(LANES, -1)
_C_LANE_DIM[0:_C_HALF] = C_NOPE + np.arange(_C_HALF)
_C_LANE_DIM[_C_HALF:_C_HALF + 32] = np.arange(32)
_C_LANE_DIM[64:64 + _C_HALF] = C_NOPE + _C_HALF + np.arange(_C_HALF)
_C_LANE_DIM[64 + _C_HALF:64 + _C_HALF + 32] = 32 + np.arange(32)


def _place_heads(w, per_head, lane_dim):
    idx = (np.arange(C_HEADS)[:, None] * per_head + np.maximum(lane_dim, 0)[None, :]).reshape(-1)
    keep = np.tile(lane_dim >= 0, C_HEADS)
    return jnp.where(jnp.asarray(keep)[None, :], w[:, idx], 0.0)


def _layer1_rope(seq):
    cos, sin = _rope_tables(seq, C_ROPE)
    ones = jnp.ones((seq, 64 - _C_HALF), _f32)
    zeros = jnp.zeros((seq, 64 - _C_HALF), _f32)
    return (jnp.concatenate([cos, ones, cos, ones], axis=1),
            jnp.concatenate([-sin, zeros, sin, zeros], axis=1))


def _layer1_group_matrix():
    lane = np.arange(MXU_N)
    return jnp.asarray((lane[:, None] // LANES) == (lane[None, :] // LANES), _bf16)


def _trunk(x3, e_norm, e_w_in, e_a_q_gain, e_a_k_gain, e_a_sink, e_b_q_gain, e_b_k_gain, e_w_out,
           o_norm, o_w_in, o_q_lora_gain, o_w_uq, o_kv_gain, o_w_ukv, o_q_gain, o_k_gain, o_w_out,
           f_norm, f_w_up, f_conv_w, f_conv_b, f_w_down):
    bsz, seq, _ = x3.shape
    assert seq % TM == 0
    x = x3.reshape(bsz * seq, D_MODEL)
    for layer in range(DEPTH):
        i = layer // 2
        if layer % 2 == 0:
            w0 = e_w_in[i][:, _layer0_columns()].astype(_bf16)
            cos, sin = _layer0_rope(seq)
            outs = _proj0(x, seq, e_norm[i][None, :], w0,
                          _layer0_gains(e_a_q_gain[i], e_a_k_gain[i], e_b_q_gain[i], e_b_k_gain[i]),
                          _layer0_group_matrix(), cos, sin)
            qa, kd, vd = outs[:3]
            a_out = _local_attn(qa.reshape(bsz, seq, A_Q_DIM), kd.reshape(bsz, seq, MXU_N),
                                vd.reshape(bsz, seq, MXU_N), e_a_sink[i].reshape(A_Q_HEADS // 2, 2),
                                half=A_HALF_WINDOW, kv_of_pair=lambda p: p // 2, want_lse=False)[0]
            b_outs, b_lses = [], []
            for g, (window, r) in enumerate(B_CONFIGS):
                bq, bk, bv = (t.reshape(bsz, seq // r, r * B_DIM) for t in outs[3 + 3 * g:6 + 3 * g])
                o, lse = _local_attn(bq, bk, bv, None, half=(window // 2) // r,
                                     kv_of_pair=lambda p: p, want_lse=True)
                b_outs.append(o.reshape(bsz * seq // r, r * B_DIM))
                b_lses.append(lse.reshape(bsz * seq // r, r * B_DIM))
            w_out = e_w_out[i].astype(_bf16)
            x = _out0(x, a_out.reshape(bsz * seq, A_Q_DIM), b_outs, b_lses,
                      w_out[:A_Q_DIM], w_out[A_Q_DIM:])
        else:
            w_in = o_w_in[i]
            k_rope_cols = jnp.zeros((D_MODEL, LANES), _f32)
            k_rope_cols = k_rope_cols.at[:, 0:_C_HALF].set(w_in[:, C_Q_LORA + C_KV_LORA:C_Q_LORA + C_KV_LORA + _C_HALF])
            k_rope_cols = k_rope_cols.at[:, 64:64 + _C_HALF].set(w_in[:, C_Q_LORA + C_KV_LORA + _C_HALF:])
            win = jnp.concatenate([w_in[:, :C_Q_LORA + C_KV_LORA], k_rope_cols], axis=1).astype(_bf16)
            wq = _place_heads(o_w_uq[i], C_QK, _C_LANE_DIM).astype(_bf16)
            nope_only = np.where(_C_LANE_DIM < C_NOPE, _C_LANE_DIM, -1)
            wk = _place_heads(o_w_ukv[i], C_NOPE + C_V, nope_only).astype(_bf16)
            r_in_pair = np.arange(C_HEADS * LANES) % MXU_N
            head = np.arange(C_HEADS * LANES) // LANES
            is_value = (r_in_pair < C_V) | (r_in_pair >= MXU_N - C_V)
            v_col = head * (C_NOPE + C_V) + C_NOPE + r_in_pair % C_V
            wvt = jnp.where(jnp.asarray(is_value)[:, None], o_w_ukv[i][:, v_col].T, 0.0).astype(_bf16)
            lane_ok = jnp.asarray(_C_LANE_DIM >= 0)
            q_scale = (C_QK ** -0.5) * np.log2(np.e)
            qg = jnp.where(lane_ok, o_q_gain[i][np.maximum(_C_LANE_DIM, 0)] * q_scale, 0.0)[None, :]
            kg = jnp.where(lane_ok, o_k_gain[i][np.maximum(_C_LANE_DIM, 0)], 0.0)[None, :]
            cos, sin = _layer1_rope(seq)
            q, k, vt = _proj1(x, seq, o_norm[i][None, :], win, o_q_lora_gain[i][None, :],
                              o_kv_gain[i][None, :], wq, wk, wvt, qg, kg, _layer1_group_matrix(),
                              cos, sin)
            o = _mla_attn(q.reshape(bsz, seq, -1), k.reshape(bsz, seq, -1), vt)
            x = _out1(x, o.reshape(bsz * seq, C_HEADS * C_V), o_w_out[i].astype(_bf16))
        w_up = f_w_up[layer]
        wg = w_up[:, :D_FF].reshape(D_MODEL, N_FF_CHUNK, FF_CHUNK).transpose(1, 0, 2).astype(_bf16)
        wv_ = w_up[:, D_FF:].reshape(D_MODEL, N_FF_CHUNK, FF_CHUNK).transpose(1, 0, 2).astype(_bf16)
        cw = jnp.concatenate([f_conv_w[layer], f_conv_b[layer][None, :], jnp.zeros((4, D_FF), _f32)], axis=0)
        cw = cw.reshape(8, N_FF_CHUNK, FF_CHUNK).transpose(1, 0, 2)
        wd = f_w_down[layer].reshape(N_FF_CHUNK, FF_CHUNK, D_MODEL).astype(_bf16)
        x = _ffn(x, seq, f_norm[layer][None, :], wg, wv_, cw, wd)
    return x.reshape(bsz, seq, D_MODEL)


def kernel(x_prompt, x_sample, e_norm, e_w_in, e_a_q_gain, e_a_k_gain, e_a_sink, e_b_q_gain, e_b_k_gain, e_w_out, o_norm, o_w_in, o_q_lora_gain, o_w_uq, o_kv_gain, o_w_ukv, o_q_gain, o_k_gain, o_w_out, f_norm, f_w_up, f_conv_w, f_conv_b, f_w_down):
    weights = (e_norm, e_w_in, e_a_q_gain, e_a_k_gain, e_a_sink, e_b_q_gain, e_b_k_gain, e_w_out,
               o_norm, o_w_in, o_q_lora_gain, o_w_uq, o_kv_gain, o_w_ukv, o_q_gain, o_k_gain, o_w_out,
               f_norm, f_w_up, f_conv_w, f_conv_b, f_w_down)
    return (_trunk(x_prompt, *weights), _trunk(x_sample, *weights))
```

```python
import functools

import numpy as np
import jax
import jax.numpy as jnp
from jax import lax
from jax.experimental import pallas as pl
from jax.experimental.pallas import tpu as pltpu

D_MODEL = 1024
HEAD_DIM = 64
ROPE_THETA = 10000.0
EPS = 1e-6
NEG = -1e30
A_Q_HEADS = 8
A_KV_HEADS = 2
A_HALF_WINDOW = 128
B_HEADS = 4
B_CONFIGS = ((128, 1), (512, 4), (2048, 16))
N_BRANCH = len(B_CONFIGS)
A_Q_DIM = A_Q_HEADS * HEAD_DIM
A_KV_DIM = A_KV_HEADS * HEAD_DIM
B_DIM = B_HEADS * HEAD_DIM
C_HEADS = 16
C_NOPE = 64
C_ROPE = 32
C_QK = C_NOPE + C_ROPE
C_V = 64
C_Q_LORA = 256
C_KV_LORA = 256
D_FF = 2816
DEPTH = 2

LANES = 128
MXU_N = 256
BF16_ROWS = 16
VMEM_LIMIT = 56 * 1024 * 1024
C_VROWS = C_V + BF16_ROWS

TM = 512
ATT_TQ = 128
ATT_UNROLL = 4
MLA_TQ = 512
MLA_TK = 512
MLA_UNROLL = 4
FF_CHUNK = 256
N_FF_CHUNK = D_FF // FF_CHUNK
HALO = BF16_ROWS

NR_BLOCKS = 9
P0_COLS = 13 * MXU_N

LOG2_E = float(np.log2(np.e))

_f32 = jnp.float32
_bf16 = jnp.bfloat16


def _const_spec(shape):
    nd = len(shape)
    return pl.BlockSpec(shape, lambda *_: (0,) * nd, pipeline_mode=pl.Buffered(1))


def _params(n_axes):
    return pltpu.CompilerParams(dimension_semantics=("arbitrary",) * n_axes,
                                vmem_limit_bytes=VMEM_LIMIT)


def _rms(x, g):
    y = x * lax.rsqrt(jnp.mean(x * x, axis=-1, keepdims=True) + EPS)
    return y * g


def _group_sumsq(a, gm):
    sq = a * a
    hi = sq.astype(_bf16)
    lo = (sq - hi.astype(_f32)).astype(_bf16)
    return (jnp.dot(hi, gm, preferred_element_type=_f32)
            + jnp.dot(lo, gm, preferred_element_type=_f32))


def _norm_rope(a, gm, gain, cos, sin, inv_dim):
    ss = _group_sumsq(a, gm)
    y = a * lax.rsqrt(ss * inv_dim + EPS) * gain
    halves = []
    for t in range(2):
        yt = y[:, t * LANES:(t + 1) * LANES]
        halves.append(yt * cos + pltpu.roll(yt, LANES // 2, 1) * sin)
    return halves


def _proj0_kernel(x_ref, g_ref, w_ref, hg_ref, gm_ref, cos_ref, sin_ref,
                  qa_ref, kd_ref, vd_ref,
                  bq0_ref, bk0_ref, bv0_ref, bq1_ref, bk1_ref, bv1_ref, bq2_ref, bk2_ref, bv2_ref,
                  h_scr, fold_scr):
    tm = x_ref.shape[0]
    h_scr[...] = _rms(x_ref[...], g_ref[...]).astype(_bf16)
    cos = cos_ref[...]
    sin = sin_ref[...]
    gm = gm_ref[...]

    blocks = {}

    def matmul_block(j):
        if j not in blocks:
            n_blk = min(2, P0_COLS // MXU_N - j)
            a = jnp.dot(h_scr[...], w_ref[:, j * MXU_N:(j + n_blk) * MXU_N], preferred_element_type=_f32)
            for t in range(n_blk):
                blocks[j + t] = a[:, t * MXU_N:(t + 1) * MXU_N]
        return blocks.pop(j)

    def store_folded(out_ref, val, r):
        if r == 1:
            out_ref[...] = val.astype(_bf16)
            return
        for t in range(2):
            fold_scr[t] = val[:, t * LANES:(t + 1) * LANES]
        for c in range(r):
            for t in range(2):
                lo = c * MXU_N + t * LANES
                out_ref[:, lo:lo + LANES] = fold_scr[t, pl.ds(c, tm // r, stride=r), :].astype(_bf16)

    def nr_block(j):
        lo, hi = _norm_rope(matmul_block(j), gm, hg_ref[:, j * MXU_N:(j + 1) * MXU_N],
                            cos, sin, 1.0 / HEAD_DIM)
        return jnp.concatenate([lo, hi], axis=1)

    qa_ref[:, 0:MXU_N] = nr_block(0).astype(_bf16)
    qa_ref[:, MXU_N:2 * MXU_N] = nr_block(1).astype(_bf16)
    kd_ref[...] = nr_block(2).astype(_bf16)
    b_refs = ((bq0_ref, bk0_ref, bv0_ref), (bq1_ref, bk1_ref, bv1_ref), (bq2_ref, bk2_ref, bv2_ref))
    for g, (_, r) in enumerate(B_CONFIGS):
        store_folded(b_refs[g][0], nr_block(3 + 2 * g), r)
        store_folded(b_refs[g][1], nr_block(4 + 2 * g), r)
    vd_ref[...] = matmul_block(NR_BLOCKS).astype(_bf16)
    for g, (_, r) in enumerate(B_CONFIGS):
        store_folded(b_refs[g][2], matmul_block(NR_BLOCKS + 1 + g), r)


def _proj0(x, seq, norm_g, w, head_gain, gm, cos, sin):
    t_rows = x.shape[0]
    tm = TM
    n_pos_blocks = seq // tm
    row = lambda i: (i, 0)
    out_shapes = [jax.ShapeDtypeStruct((t_rows, A_Q_DIM), _bf16),
                  jax.ShapeDtypeStruct((t_rows, MXU_N), _bf16),
                  jax.ShapeDtypeStruct((t_rows, MXU_N), _bf16)]
    out_specs = [pl.BlockSpec((tm, A_Q_DIM), row), pl.BlockSpec((tm, MXU_N), row),
                 pl.BlockSpec((tm, MXU_N), row)]
    for _, r in B_CONFIGS:
        for _ in range(3):
            out_shapes.append(jax.ShapeDtypeStruct((t_rows // r, r * B_DIM), _bf16))
            out_specs.append(pl.BlockSpec((tm // r, r * B_DIM), row))
    return pl.pallas_call(
        _proj0_kernel,
        name="proj0",
        grid=(t_rows // tm,),
        in_specs=[pl.BlockSpec((tm, D_MODEL), row),
                  _const_spec((1, D_MODEL)),
                  _const_spec((D_MODEL, P0_COLS)),
                  _const_spec((1, NR_BLOCKS * MXU_N)),
                  _const_spec((MXU_N, MXU_N)),
                  pl.BlockSpec((tm, LANES), lambda i: (i % n_pos_blocks, 0)),
                  pl.BlockSpec((tm, LANES), lambda i: (i % n_pos_blocks, 0))],
        out_specs=out_specs,
        out_shape=out_shapes,
        scratch_shapes=[pltpu.VMEM((tm, D_MODEL), _bf16), pltpu.VMEM((2, tm, LANES), _f32)],
        compiler_params=_params(1),
    )(x, norm_g, w, head_gain, gm, cos, sin)


def _local_attn_kernel(sink_ref, q_ref, k_ref, v_ref, *refs, seq, tq, half, has_sink, want_lse):
    o_ref = refs[0]
    bias_scr = refs[-1]
    pair = pl.program_id(1)
    win = tq + 2 * half
    lane = lax.broadcasted_iota(jnp.int32, (1, LANES), 1)
    first_qk = ((lane // 32) % 2) == 0
    first_v = lane < HEAD_DIM
    row = lax.broadcasted_iota(jnp.int32, (2 * tq, 1), 0)
    top = row < tq
    qoff = jnp.where(top, row, row - tq)
    koff = lax.broadcasted_iota(jnp.int32, (1, win), 1)
    if has_sink:
        sink = jnp.where(top, sink_ref[pair, 0], sink_ref[pair, 1]) * LOG2_E
    for e in range(3):
        bias_scr[e] = jnp.where(jnp.abs(e * half + qoff - koff) <= half, 0.0, NEG)

    def step(g, carry):
        tiles = range(n_unroll)
        qs = [pl.multiple_of((g * n_unroll + u) * tq, tq) for u in tiles]
        ws = [pl.multiple_of(jnp.clip(qs[u] - half, 0, seq - win), half) for u in tiles]
        s = []
        for u in tiles:
            q2 = q_ref[0, pl.ds(qs[u], tq), :]
            zero = jnp.zeros_like(q2)
            qq = jnp.concatenate([jnp.where(first_qk, q2, zero), jnp.where(first_qk, zero, q2)], axis=0)
            kw = k_ref[0, pl.ds(ws[u], win), :]
            s.append(lax.dot_general(qq, kw, (((1,), (1,)), ((), ())), preferred_element_type=_f32))
        s = [s[u] + bias_scr[(qs[u] - ws[u]) // half] for u in tiles]
        m = [jnp.max(s[u], axis=-1, keepdims=True) for u in tiles]
        if has_sink:
            m = [jnp.maximum(m[u], sink) for u in tiles]
        p = [jnp.exp2(s[u] - m[u]) for u in tiles]
        den = [jnp.sum(p[u], axis=-1, keepdims=True) for u in tiles]
        if has_sink:
            den = [den[u] + jnp.exp2(sink - m[u]) for u in tiles]
        pv = [jnp.dot(p[u].astype(_bf16), v_ref[0, pl.ds(ws[u], win), :], preferred_element_type=_f32)
              for u in tiles]
        for u in tiles:
            od = pv[u] / den[u]
            o_ref[0, pl.ds(qs[u], tq), :] = jnp.where(first_v, od[:tq], od[tq:]).astype(o_ref.dtype)
            if want_lse:
                lse = m[u] + jnp.log2(den[u])
                refs[1][0, pl.ds(qs[u], tq), :] = jnp.where(first_v, lse[:tq], lse[tq:])
        return carry

    n_unroll = min(ATT_UNROLL, seq // tq)
    lax.fori_loop(0, seq // (tq * n_unroll), step, 0)


def _local_attn(q, k, v, sink, *, half, kv_of_pair, want_lse):
    bsz, seq, width = q.shape
    n_pairs = width // LANES
    tq = ATT_TQ
    assert seq % (tq * min(ATT_UNROLL, seq // tq)) == 0 and seq >= tq + 2 * half
    has_sink = sink is not None
    if not has_sink:
        sink = jnp.zeros((1, 2), _f32)
    qmap = lambda b, p: (b, 0, p)
    kmap = lambda b, p: (b, 0, kv_of_pair(p))
    out_shape = [jax.ShapeDtypeStruct((bsz, seq, width), _bf16)]
    out_specs = [pl.BlockSpec((1, seq, LANES), qmap)]
    if want_lse:
        out_shape.append(jax.ShapeDtypeStruct((bsz, seq, width), _f32))
        out_specs.append(pl.BlockSpec((1, seq, LANES), qmap))
    return pl.pallas_call(
        functools.partial(_local_attn_kernel, seq=seq, tq=tq, half=half, has_sink=has_sink,
                          want_lse=want_lse),
        name="local_attn_h%d" % half,
        grid=(bsz, n_pairs),
        in_specs=[pl.BlockSpec(memory_space=pltpu.SMEM),
                  pl.BlockSpec((1, seq, LANES), qmap),
                  pl.BlockSpec((1, seq, LANES), kmap),
                  pl.BlockSpec((1, seq, LANES), kmap)],
        out_specs=out_specs,
        out_shape=out_shape,
        scratch_shapes=[pltpu.VMEM((3, 2 * tq, tq + 2 * half), _f32)],
        compiler_params=_params(2),
    )(sink, q, k, v)


def _out0_kernel(x_ref, a_ref, o0_ref, l0_ref, o1_ref, l1_ref, o2_ref, l2_ref, wa_ref, wb_ref,
                 y_ref, o_scr, l_scr):
    tm = x_ref.shape[0]

    def unfold(src_ref, dst_scr, r):
        if r == 1:
            return src_ref[...].astype(_f32)
        for c in range(r):
            for t in range(2):
                lo = c * MXU_N + t * LANES
                dst_scr[t, pl.ds(c, tm // r, stride=r), :] = src_ref[:, lo:lo + LANES].astype(_f32)
        return jnp.concatenate([dst_scr[0], dst_scr[1]], axis=1)

    outs, lses = [], []
    for (o_ref, l_ref), (_, r) in zip(((o0_ref, l0_ref), (o1_ref, l1_ref), (o2_ref, l2_ref)), B_CONFIGS):
        outs.append(unfold(o_ref, o_scr, r))
        lses.append(unfold(l_ref, l_scr, r))
    m = jnp.maximum(jnp.maximum(lses[0], lses[1]), lses[2])
    es = [jnp.exp2(l - m) for l in lses]
    tot = es[0] + es[1] + es[2]
    b_out = (es[0] / tot) * outs[0] + (es[1] / tot) * outs[1] + (es[2] / tot) * outs[2]
    y = x_ref[...] + jnp.dot(a_ref[...], wa_ref[...], preferred_element_type=_f32)
    y_ref[...] = y + jnp.dot(b_out.astype(_bf16), wb_ref[...], preferred_element_type=_f32)


def _out0(x, a_out, b_outs, b_lses, wa, wb):
    t_rows = x.shape[0]
    tm = TM
    row = lambda i: (i, 0)
    in_specs = [pl.BlockSpec((tm, D_MODEL), row), pl.BlockSpec((tm, A_Q_DIM), row)]
    args = [x, a_out]
    for (_, r), o, l in zip(B_CONFIGS, b_outs, b_lses):
        in_specs += [pl.BlockSpec((tm // r, r * B_DIM), row)] * 2
        args += [o, l]
    in_specs += [_const_spec((A_Q_DIM, D_MODEL)), _const_spec((B_DIM, D_MODEL))]
    args += [wa, wb]
    return pl.pallas_call(
        _out0_kernel,
        name="out0",
        grid=(t_rows // tm,),
        in_specs=in_specs,
        out_specs=pl.BlockSpec((tm, D_MODEL), row),
        out_shape=jax.ShapeDtypeStruct((t_rows, D_MODEL), _f32),
        scratch_shapes=[pltpu.VMEM((2, tm, LANES), _f32), pltpu.VMEM((2, tm, LANES), _f32)],
        compiler_params=_params(1),
    )(*args)


def _ffn_kernel(xp_ref, x_ref, xn_ref, g_ref, wg_ref, wv_ref, cw_ref, wd_ref, y_ref,
                h_scr, gate_a, gate_b, val_a, val_b, acc_scr, *, tiles_per_seq):
    tm = x_ref.shape[0]
    i = pl.program_id(0)
    pos = i % tiles_per_seq
    keep_prev = jnp.where(pos == 0, 0.0, 1.0)
    keep_next = jnp.where(pos == tiles_per_seq - 1, 0.0, 1.0)
    g = g_ref[...]
    h_scr[0:HALO, :] = _rms(xp_ref[...], g).astype(_bf16)
    h_scr[HALO:HALO + tm, :] = _rms(x_ref[...], g).astype(_bf16)
    h_scr[HALO + tm:, :] = _rms(xn_ref[...], g).astype(_bf16)

    def up(j, gate_scr, val_scr):
        gate_scr[...] = jnp.dot(h_scr[...], wg_ref[j], preferred_element_type=_f32)
        gate_scr[HALO - 8:HALO, :] = gate_scr[HALO - 8:HALO, :] * keep_prev
        gate_scr[HALO + tm:HALO + tm + 8, :] = gate_scr[HALO + tm:HALO + tm + 8, :] * keep_next
        val_scr[...] = jnp.dot(h_scr[HALO:HALO + tm, :], wv_ref[j], preferred_element_type=_f32)

    def act(j, gate_scr, val_scr):
        cw = cw_ref[j]
        conv = (cw[0:1] * gate_scr[HALO - 1:HALO - 1 + tm, :]
                + cw[1:2] * gate_scr[HALO:HALO + tm, :]
                + cw[2:3] * gate_scr[HALO + 1:HALO + 1 + tm, :]
                + cw[3:4])
        return (jax.nn.gelu(conv) * val_scr[...]).astype(_bf16)

    def step(jj, carry):
        j = 2 * jj
        up(j + 1, gate_b, val_b)
        a0 = act(j, gate_a, val_a)
        up(j + 2, gate_a, val_a)
        a1 = act(j + 1, gate_b, val_b)
        acc_scr[...] += (jnp.dot(a0, wd_ref[j], preferred_element_type=_f32)
                         + jnp.dot(a1, wd_ref[j + 1], preferred_element_type=_f32))
        return carry

    acc_scr[...] = jnp.zeros_like(acc_scr)
    up(0, gate_a, val_a)
    lax.fori_loop(0, (N_FF_CHUNK - 1) // 2, step, 0)
    a_last = act(N_FF_CHUNK - 1, gate_a, val_a)
    y_ref[...] = x_ref[...] + (acc_scr[...]
                               + jnp.dot(a_last, wd_ref[N_FF_CHUNK - 1], preferred_element_type=_f32))


def _ffn(x, seq, norm_g, wg, wv, cw, wd):
    t_rows = x.shape[0]
    tm = TM
    per = tm // HALO
    n_halo_blocks = t_rows // HALO
    return pl.pallas_call(
        functools.partial(_ffn_kernel, tiles_per_seq=seq // tm),
        name="conv_ffn",
        grid=(t_rows // tm,),
        in_specs=[pl.BlockSpec((HALO, D_MODEL), lambda i: (jnp.maximum(i * per - 1, 0), 0)),
                  pl.BlockSpec((tm, D_MODEL), lambda i: (i, 0)),
                  pl.BlockSpec((HALO, D_MODEL), lambda i: (jnp.minimum((i + 1) * per, n_halo_blocks - 1), 0)),
                  _const_spec((1, D_MODEL)),
                  _const_spec((N_FF_CHUNK, D_MODEL, FF_CHUNK)),
                  _const_spec((N_FF_CHUNK, D_MODEL, FF_CHUNK)),
                  _const_spec((N_FF_CHUNK, 8, FF_CHUNK)),
                  _const_spec((N_FF_CHUNK, FF_CHUNK, D_MODEL))],
        out_specs=pl.BlockSpec((tm, D_MODEL), lambda i: (i, 0)),
        out_shape=jax.ShapeDtypeStruct((t_rows, D_MODEL), _f32),
        scratch_shapes=[pltpu.VMEM((tm + 2 * HALO, D_MODEL), _bf16),
                        pltpu.VMEM((tm + 2 * HALO, FF_CHUNK), _f32),
                        pltpu.VMEM((tm + 2 * HALO, FF_CHUNK), _f32),
                        pltpu.VMEM((tm, FF_CHUNK), _f32),
                        pltpu.VMEM((tm, FF_CHUNK), _f32),
                        pltpu.VMEM((tm, D_MODEL), _f32)],
        compiler_params=_params(1),
    )(x, x, x, norm_g, wg, wv, cw, wd)


def _proj1_kernel(x_ref, g_ref, win_ref, qlg_ref, kvg_ref, wq_ref, wk_ref, wvt_ref, qg_ref, kg_ref,
                  gm_ref, cos_ref, sin_ref, q_ref, k_ref, vt_ref):
    tm = x_ref.shape[0]
    h = _rms(x_ref[...], g_ref[...]).astype(_bf16)
    proj = jnp.dot(h, win_ref[...], preferred_element_type=_f32)
    cq = _rms(proj[:, :C_Q_LORA], qlg_ref[...]).astype(_bf16)
    ckv_f32 = _rms(proj[:, C_Q_LORA:C_Q_LORA + C_KV_LORA], kvg_ref[...])
    ckv = ckv_f32.astype(_bf16)
    ckv_t = ckv_f32.T.astype(_bf16)
    k_rope = proj[:, C_Q_LORA + C_KV_LORA:]
    k_rope2 = jnp.concatenate([k_rope, k_rope], axis=1)
    cos = cos_ref[...]
    sin = sin_ref[...]
    gm = gm_ref[...]
    qg = jnp.concatenate([qg_ref[...]] * 2, axis=1)
    kg = jnp.concatenate([kg_ref[...]] * 2, axis=1)
    for b in range(C_HEADS // 2):
        cols = slice(b * MXU_N, (b + 1) * MXU_N)
        aq = jnp.dot(cq, wq_ref[:, cols], preferred_element_type=_f32)
        lo, hi = _norm_rope(aq, gm, qg, cos, sin, 1.0 / C_QK)
        q_ref[:, cols] = jnp.concatenate([lo, hi], axis=1).astype(_bf16)
        ak = jnp.dot(ckv, wk_ref[:, cols], preferred_element_type=_f32) + k_rope2
        lo, hi = _norm_rope(ak, gm, kg, cos, sin, 1.0 / C_QK)
        k_ref[:, cols] = jnp.concatenate([lo, hi], axis=1).astype(_bf16)
    r = lax.broadcasted_iota(jnp.int32, (MXU_N, tm), 0)
    for b in range(C_HEADS * C_VROWS // MXU_N):
        rows = slice(b * MXU_N, (b + 1) * MXU_N)
        vt = jnp.dot(wvt_ref[rows, :], ckv_t, preferred_element_type=_f32)
        ones_rows = (r + b * MXU_N) % C_VROWS >= C_V
        vt_ref[0, rows, :] = jnp.where(ones_rows, 1.0, vt).astype(_bf16)


def _proj1(x, seq, norm_g, win, qlg, kvg, wq, wk, wvt, qg, kg, gm, cos, sin):
    t_rows = x.shape[0]
    tm = TM
    n_pos_blocks = seq // tm
    row = lambda i: (i, 0)
    qk_w = C_HEADS * LANES
    v_w = C_HEADS * C_VROWS
    return pl.pallas_call(
        _proj1_kernel,
        name="proj1",
        grid=(t_rows // tm,),
        in_specs=[pl.BlockSpec((tm, D_MODEL), row),
                  _const_spec((1, D_MODEL)),
                  _const_spec(win.shape),
                  _const_spec((1, C_Q_LORA)), _const_spec((1, C_KV_LORA)),
                  _const_spec((C_Q_LORA, qk_w)), _const_spec((C_KV_LORA, qk_w)),
                  _const_spec((v_w, C_KV_LORA)),
                  _const_spec((1, LANES)), _const_spec((1, LANES)),
                  _const_spec((MXU_N, MXU_N)),
                  pl.BlockSpec((tm, LANES), lambda i: (i % n_pos_blocks, 0)),
                  pl.BlockSpec((tm, LANES), lambda i: (i % n_pos_blocks, 0))],
        out_specs=[pl.BlockSpec((tm, qk_w), row), pl.BlockSpec((tm, qk_w), row),
                   pl.BlockSpec((1, v_w, tm), lambda i: (i, 0, 0))],
        out_shape=[jax.ShapeDtypeStruct((t_rows, qk_w), _bf16),
                   jax.ShapeDtypeStruct((t_rows, qk_w), _bf16),
                   jax.ShapeDtypeStruct((t_rows // tm, v_w, tm), _bf16)],
        compiler_params=_params(1),
    )(x, norm_g, win, qlg, kvg, wq, wk, wvt, qg, kg, gm, cos, sin)


def _mla_kernel(q_ref, k_ref, vt_ref, o_ref, sa_ref, sb_ref, *, seq, tk, unroll):
    tq = q_ref.shape[1]
    n_tiles = seq // tk
    s_bufs = (sa_ref, sb_ref)

    def scores(t, buf, hh):
        ks = pl.multiple_of(t * tk, tk)
        kt = k_ref[0, pl.ds(ks, tk), hh * LANES:(hh + 1) * LANES]
        qh = q_ref[0, :, hh * LANES:(hh + 1) * LANES]
        s = lax.dot_general(kt, qh, (((1,), (1,)), ((), ())), preferred_element_type=_f32)
        buf[hh] = s
        return jnp.max(s, axis=0, keepdims=True)

    def consume(t, buf, hh, tmax, state):
        m_old, acc = state
        m_new = jnp.maximum(m_old, tmax)
        alpha = jnp.exp2(m_old - m_new)
        p = jnp.exp2(buf[hh] - m_new).astype(_bf16)
        vt = vt_ref[t, hh * C_VROWS:(hh + 1) * C_VROWS, :]
        acc = alpha * acc + jnp.dot(vt, p, preferred_element_type=_f32)
        return m_new, acc

    def run_tiles(t0, tmax, state, score_following):
        tmax, state = list(tmax), list(state)
        for u in range(unroll):
            score_next = u < unroll - 1 or score_following
            for hh in range(2):
                if score_next:
                    tmax_next = scores(t0 + u + 1, s_bufs[(u + 1) % 2], hh)
                state[hh] = consume(t0 + u, s_bufs[u % 2], hh, tmax[hh], state[hh])
                if score_next:
                    tmax[hh] = tmax_next
        return tuple(tmax), tuple(state)

    state = tuple((jnp.full((1, tq), -jnp.inf, _f32), jnp.zeros((C_VROWS, tq), _f32)) for _ in range(2))
    tmax = tuple(scores(0, s_bufs[0], hh) for hh in range(2))
    tmax, state = lax.fori_loop(
        0, n_tiles // unroll - 1,
        lambda jj, c: run_tiles(unroll * jj, c[0], c[1], True), (tmax, state))
    _, ((_, acc0), (_, acc1)) = run_tiles(n_tiles - unroll, tmax, state, False)
    out_t = jnp.concatenate([acc0[:C_V] / acc0[C_V:C_V + 1], acc1[:C_V] / acc1[C_V:C_V + 1]], axis=0)
    o_ref[0] = out_t.T.astype(o_ref.dtype)


def _mla_attn(q, k, vt):
    bsz, seq, _ = q.shape
    n_pairs = C_HEADS // 2
    tq, tk = MLA_TQ, MLA_TK
    n_tiles = seq // tk
    unroll = MLA_UNROLL if n_tiles >= 3 * MLA_UNROLL else 2
    assert vt.shape[2] == tk and n_tiles % unroll == 0 and seq % tq == 0
    return pl.pallas_call(
        functools.partial(_mla_kernel, seq=seq, tk=tk, unroll=unroll),
        name="mla_attn",
        grid=(bsz, n_pairs, seq // tq),
        in_specs=[pl.BlockSpec((1, tq, 2 * LANES), lambda b, p, i: (b, i, p)),
                  pl.BlockSpec((1, seq, 2 * LANES), lambda b, p, i: (b, 0, p)),
                  pl.BlockSpec((seq // tk, 2 * C_VROWS, tk), lambda b, p, i: (b, p, 0))],
        out_specs=pl.BlockSpec((1, tq, LANES), lambda b, p, i: (b, i, p)),
        out_shape=jax.ShapeDtypeStruct((bsz, seq, C_HEADS * C_V), _bf16),
        scratch_shapes=[pltpu.VMEM((2, tk, tq), _f32), pltpu.VMEM((2, tk, tq), _f32)],
        compiler_params=_params(3),
    )(q, k, vt)


def _out1_kernel(x_ref, o_ref, w_ref, y_ref):
    y_ref[...] = x_ref[...] + jnp.dot(o_ref[...], w_ref[...], preferred_element_type=_f32)


def _out1(x, o, w):
    t_rows = x.shape[0]
    tm = TM
    row = lambda i: (i, 0)
    return pl.pallas_call(
        _out1_kernel,
        name="out1",
        grid=(t_rows // tm,),
        in_specs=[pl.BlockSpec((tm, D_MODEL), row), pl.BlockSpec((tm, o.shape[1]), row),
                  _const_spec(w.shape)],
        out_specs=pl.BlockSpec((tm, D_MODEL), row),
        out_shape=jax.ShapeDtypeStruct((t_rows, D_MODEL), _f32),
        compiler_params=_params(1),
    )(x, o, w)


def _pair_cols(a, b):
    r = np.arange(HEAD_DIM // 2)
    return np.concatenate([a + r, b + r, a + HEAD_DIM // 2 + r, b + HEAD_DIM // 2 + r])


_PAIR_DIMS = _pair_cols(0, 0)


def _layer0_columns():
    nr, val = [], []
    for p in range(A_Q_HEADS // 2):
        nr.append(_pair_cols(2 * p * HEAD_DIM, (2 * p + 1) * HEAD_DIM))
    for h in range(A_KV_HEADS):
        nr.append(_pair_cols(A_Q_DIM + h * HEAD_DIM, A_Q_DIM + h * HEAD_DIM))
    b0 = A_Q_DIM + 2 * A_KV_DIM
    for g in range(N_BRANCH):
        for t in range(2):
            base = b0 + g * 3 * B_DIM + t * B_DIM
            for p in range(B_HEADS // 2):
                nr.append(_pair_cols(base + 2 * p * HEAD_DIM, base + (2 * p + 1) * HEAD_DIM))
    d = np.arange(HEAD_DIM)
    for h in range(A_KV_HEADS):
        base = A_Q_DIM + A_KV_DIM + h * HEAD_DIM
        val.append(np.concatenate([base + d, base + d]))
    for g in range(N_BRANCH):
        val.append(b0 + g * 3 * B_DIM + 2 * B_DIM + np.arange(B_DIM))
    return np.concatenate(nr + val)


def _layer0_gains(a_q_gain, a_k_gain, b_q_gain, b_k_gain):
    scale = (HEAD_DIM ** -0.5) * LOG2_E
    aq = a_q_gain[_PAIR_DIMS] * scale
    ak = a_k_gain[_PAIR_DIMS]
    parts = [aq] * (A_Q_HEADS // 2) + [ak] * A_KV_HEADS
    for g in range(N_BRANCH):
        parts += [b_q_gain[g][_PAIR_DIMS] * scale] * (B_HEADS // 2)
        parts += [b_k_gain[g][_PAIR_DIMS]] * (B_HEADS // 2)
    return jnp.concatenate(parts)[None, :]


def _layer0_group_matrix():
    lane = np.arange(MXU_N)
    head = (lane // LANES) * 2 + (lane // 32) % 2
    return jnp.asarray(head[:, None] == head[None, :], _bf16)


def _rope_tables(seq, dim):
    inv = jnp.power(ROPE_THETA, -jnp.arange(0, dim, 2, dtype=_f32) / dim)
    ang = jnp.arange(seq, dtype=_f32)[:, None] * inv[None, :]
    return jnp.cos(ang), jnp.sin(ang)


def _layer0_rope(seq):
    cos, sin = _rope_tables(seq, HEAD_DIM)
    return (jnp.concatenate([cos, cos, cos, cos], axis=1),
            jnp.concatenate([-sin, -sin, sin, sin], axis=1))


_C_HALF = C_ROPE // 2
_C_LANE_DIM = np.full(LANES, -1)
_C_LANE_DIM[0:_C_HALF] = C_NOPE + np.arange(_C_HALF)
_C_LANE_DIM[_C_HALF:_C_HALF + 32] = np.arange(32)
_C_LANE_DIM[64:64 + _C_HALF] = C_NOPE + _C_HALF + np.arange(_C_HALF)
_C_LANE_DIM[64 + _C_HALF:64 + _C_HALF + 32] = 32 + np.arange(32)


def _place_heads(w, per_head, lane_dim):
    idx = (np.arange(C_HEADS)[:, None] * per_head + np.maximum(lane_dim, 0)[None, :]).reshape(-1)
    keep = np.tile(lane_dim >= 0, C_HEADS)
    return jnp.where(jnp.asarray(keep)[None, :], w[:, idx], 0.0)


def _layer1_rope(seq):
    cos, sin = _rope_tables(seq, C_ROPE)
    ones = jnp.ones((seq, 64 - _C_HALF), _f32)
    zeros = jnp.zeros((seq, 64 - _C_HALF), _f32)
    return (jnp.concatenate([cos, ones, cos, ones], axis=1),
            jnp.concatenate([-sin, zeros, sin, zeros], axis=1))


def _layer1_group_matrix():
    lane = np.arange(MXU_N)
    return jnp.asarray((lane[:, None] // LANES) == (lane[None, :] // LANES), _bf16)


def _trunk(x3, e_norm, e_w_in, e_a_q_gain, e_a_k_gain, e_a_sink, e_b_q_gain, e_b_k_gain, e_w_out,
           o_norm, o_w_in, o_q_lora_gain, o_w_uq, o_kv_gain, o_w_ukv, o_q_gain, o_k_gain, o_w_out,
           f_norm, f_w_up, f_conv_w, f_conv_b, f_w_down):
    bsz, seq, _ = x3.shape
    assert seq % TM == 0
    x = x3.reshape(bsz * seq, D_MODEL)
    for layer in range(DEPTH):
        i = layer // 2
        if layer % 2 == 0:
            w0 = e_w_in[i][:, _layer0_columns()].astype(_bf16)
            cos, sin = _layer0_rope(seq)
            outs = _proj0(x, seq, e_norm[i][None, :], w0,
                          _layer0_gains(e_a_q_gain[i], e_a_k_gain[i], e_b_q_gain[i], e_b_k_gain[i]),
                          _layer0_group_matrix(), cos, sin)
            qa, kd, vd = outs[:3]
            a_out = _local_attn(qa.reshape(bsz, seq, A_Q_DIM), kd.reshape(bsz, seq, MXU_N),
                                vd.reshape(bsz, seq, MXU_N), e_a_sink[i].reshape(A_Q_HEADS // 2, 2),
                                half=A_HALF_WINDOW, kv_of_pair=lambda p: p // 2, want_lse=False)[0]
            b_outs, b_lses = [], []
            for g, (window, r) in enumerate(B_CONFIGS):
                bq, bk, bv = (t.reshape(bsz, seq // r, r * B_DIM) for t in outs[3 + 3 * g:6 + 3 * g])
                o, lse = _local_attn(bq, bk, bv, None, half=(window // 2) // r,
                                     kv_of_pair=lambda p: p, want_lse=True)
                b_outs.append(o.reshape(bsz * seq // r, r * B_DIM))
                b_lses.append(lse.reshape(bsz * seq // r, r * B_DIM))
            w_out = e_w_out[i].astype(_bf16)
            x = _out0(x, a_out.reshape(bsz * seq, A_Q_DIM), b_outs, b_lses,
                      w_out[:A_Q_DIM], w_out[A_Q_DIM:])
        else:
            w_in = o_w_in[i]
            k_rope_cols = jnp.zeros((D_MODEL, LANES), _f32)
            k_rope_cols = k_rope_cols.at[:, 0:_C_HALF].set(w_in[:, C_Q_LORA + C_KV_LORA:C_Q_LORA + C_KV_LORA + _C_HALF])
            k_rope_cols = k_rope_cols.at[:, 64:64 + _C_HALF].set(w_in[:, C_Q_LORA + C_KV_LORA + _C_HALF:])
            win = jnp.concatenate([w_in[:, :C_Q_LORA + C_KV_LORA], k_rope_cols], axis=1).astype(_bf16)
            wq = _place_heads(o_w_uq[i], C_QK, _C_LANE_DIM).astype(_bf16)
            nope_only = np.where(_C_LANE_DIM < C_NOPE, _C_LANE_DIM, -1)
            wk = _place_heads(o_w_ukv[i], C_NOPE + C_V, nope_only).astype(_bf16)
            r_in_head = np.arange(C_HEADS * C_VROWS) % C_VROWS
            head = np.arange(C_HEADS * C_VROWS) // C_VROWS
            v_col = head * (C_NOPE + C_V) + C_NOPE + np.minimum(r_in_head, C_V - 1)
            wvt = jnp.where(jnp.asarray(r_in_head < C_V)[:, None], o_w_ukv[i][:, v_col].T, 0.0).astype(_bf16)
            lane_ok = jnp.asarray(_C_LANE_DIM >= 0)
            q_scale = (C_QK ** -0.5) * LOG2_E
            qg = jnp.where(lane_ok, o_q_gain[i][np.maximum(_C_LANE_DIM, 0)] * q_scale, 0.0)[None, :]
            kg = jnp.where(lane_ok, o_k_gain[i][np.maximum(_C_LANE_DIM, 0)], 0.0)[None, :]
            cos, sin = _layer1_rope(seq)
            q, k, vt = _proj1(x, seq, o_norm[i][None, :], win, o_q_lora_gain[i][None, :],
                              o_kv_gain[i][None, :], wq, wk, wvt, qg, kg, _layer1_group_matrix(),
                              cos, sin)
            o = _mla_attn(q.reshape(bsz, seq, -1), k.reshape(bsz, seq, -1), vt)
            x = _out1(x, o.reshape(bsz * seq, C_HEADS * C_V), o_w_out[i].astype(_bf16))
        w_up = f_w_up[layer]
        wg = w_up[:, :D_FF].reshape(D_MODEL, N_FF_CHUNK, FF_CHUNK).transpose(1, 0, 2).astype(_bf16)
        wv_ = w_up[:, D_FF:].reshape(D_MODEL, N_FF_CHUNK, FF_CHUNK).transpose(1, 0, 2).astype(_bf16)
        cw = jnp.concatenate([f_conv_w[layer], f_conv_b[layer][None, :], jnp.zeros((4, D_FF), _f32)], axis=0)
        cw = cw.reshape(8, N_FF_CHUNK, FF_CHUNK).transpose(1, 0, 2)
        wd = f_w_down[layer].reshape(N_FF_CHUNK, FF_CHUNK, D_MODEL).astype(_bf16)
        x = _ffn(x, seq, f_norm[layer][None, :], wg, wv_, cw, wd)
    return x.reshape(bsz, seq, D_MODEL)


def kernel(x_prompt, x_sample, e_norm, e_w_in, e_a_q_gain, e_a_k_gain, e_a_sink, e_b_q_gain, e_b_k_gain, e_w_out, o_norm, o_w_in, o_q_lora_gain, o_w_uq, o_kv_gain, o_w_ukv, o_q_gain, o_k_gain, o_w_out, f_norm, f_w_up, f_conv_w, f_conv_b, f_w_down):
    weights = (e_norm, e_w_in, e_a_q_gain, e_a_k_gain, e_a_sink, e_b_q_gain, e_b_k_gain, e_w_out,
               o_norm, o_w_in, o_q_lora_gain, o_w_uq, o_kv_gain, o_w_ukv, o_q_gain, o_k_gain, o_w_out,
               f_norm, f_w_up, f_conv_w, f_conv_b, f_w_down)
    return (_trunk(x_prompt, *weights), _trunk(x_sample, *weights))
```

```python
import functools

import numpy as np
import jax
import jax.numpy as jnp
from jax import lax
from jax.experimental import pallas as pl
from jax.experimental.pallas import tpu as pltpu

D_MODEL = 1024
HEAD_DIM = 64
ROPE_THETA = 10000.0
EPS = 1e-6
NEG = -1e30
A_Q_HEADS = 8
A_KV_HEADS = 2
A_HALF_WINDOW = 128
B_HEADS = 4
B_CONFIGS = ((128, 1), (512, 4), (2048, 16))
N_BRANCH = len(B_CONFIGS)
A_Q_DIM = A_Q_HEADS * HEAD_DIM
A_KV_DIM = A_KV_HEADS * HEAD_DIM
B_DIM = B_HEADS * HEAD_DIM
C_HEADS = 16
C_NOPE = 64
C_ROPE = 32
C_QK = C_NOPE + C_ROPE
C_V = 64
C_Q_LORA = 256
C_KV_LORA = 256
D_FF = 2816
DEPTH = 2

LANES = 128
MXU_N = 256
BF16_ROWS = 16
VMEM_LIMIT = 56 * 1024 * 1024
C_VROWS = C_V + BF16_ROWS

TM = 512
ATT_TQ = 128
ATT_UNROLL = 4
MLA_TQ = 512
MLA_TK = 512
MLA_UNROLL = 4
FF_CHUNK = 256
N_FF_CHUNK = D_FF // FF_CHUNK
FF_UNROLL = 4
HALO = BF16_ROWS

NR_BLOCKS = 9
P0_COLS = 13 * MXU_N

LOG2_E = float(np.log2(np.e))

_f32 = jnp.float32
_bf16 = jnp.bfloat16


def _const_spec(shape):
    nd = len(shape)
    return pl.BlockSpec(shape, lambda *_: (0,) * nd, pipeline_mode=pl.Buffered(1))


def _params(n_axes):
    return pltpu.CompilerParams(dimension_semantics=("arbitrary",) * n_axes,
                                vmem_limit_bytes=VMEM_LIMIT)


def _rms(x, g):
    y = x * lax.rsqrt(jnp.mean(x * x, axis=-1, keepdims=True) + EPS)
    return y * g


def _group_sumsq(a, gm):
    sq = a * a
    hi = sq.astype(_bf16)
    lo = (sq - hi.astype(_f32)).astype(_bf16)
    return (jnp.dot(hi, gm, preferred_element_type=_f32)
            + jnp.dot(lo, gm, preferred_element_type=_f32))


def _norm_rope(a, gm, gain, cos, sin, inv_dim):
    ss = _group_sumsq(a, gm)
    y = a * lax.rsqrt(ss * inv_dim + EPS) * gain
    halves = []
    for t in range(2):
        yt = y[:, t * LANES:(t + 1) * LANES]
        halves.append(yt * cos + pltpu.roll(yt, LANES // 2, 1) * sin)
    return halves


def _proj0_kernel(x_ref, g_ref, w_ref, hg_ref, gm_ref, cos_ref, sin_ref,
                  qa_ref, kd_ref, vd_ref,
                  bq0_ref, bk0_ref, bv0_ref, bq1_ref, bk1_ref, bv1_ref, bq2_ref, bk2_ref, bv2_ref,
                  h_scr, fold_scr):
    tm = x_ref.shape[0]
    h_scr[...] = _rms(x_ref[...], g_ref[...]).astype(_bf16)
    cos = cos_ref[...]
    sin = sin_ref[...]
    gm = gm_ref[...]

    blocks = {}

    def matmul_block(j):
        if j not in blocks:
            n_blk = min(2, P0_COLS // MXU_N - j)
            a = jnp.dot(h_scr[...], w_ref[:, j * MXU_N:(j + n_blk) * MXU_N], preferred_element_type=_f32)
            for t in range(n_blk):
                blocks[j + t] = a[:, t * MXU_N:(t + 1) * MXU_N]
        return blocks.pop(j)

    def store_folded(out_ref, val, r):
        if r == 1:
            out_ref[...] = val.astype(_bf16)
            return
        for t in range(2):
            fold_scr[t] = val[:, t * LANES:(t + 1) * LANES]
        for c in range(r):
            for t in range(2):
                lo = c * MXU_N + t * LANES
                out_ref[:, lo:lo + LANES] = fold_scr[t, pl.ds(c, tm // r, stride=r), :].astype(_bf16)

    def nr_block(j):
        lo, hi = _norm_rope(matmul_block(j), gm, hg_ref[:, j * MXU_N:(j + 1) * MXU_N],
                            cos, sin, 1.0 / HEAD_DIM)
        return jnp.concatenate([lo, hi], axis=1)

    qa_ref[:, 0:MXU_N] = nr_block(0).astype(_bf16)
    qa_ref[:, MXU_N:2 * MXU_N] = nr_block(1).astype(_bf16)
    kd_ref[...] = nr_block(2).astype(_bf16)
    b_refs = ((bq0_ref, bk0_ref, bv0_ref), (bq1_ref, bk1_ref, bv1_ref), (bq2_ref, bk2_ref, bv2_ref))
    for g, (_, r) in enumerate(B_CONFIGS):
        store_folded(b_refs[g][0], nr_block(3 + 2 * g), r)
        store_folded(b_refs[g][1], nr_block(4 + 2 * g), r)
    vd_ref[...] = matmul_block(NR_BLOCKS).astype(_bf16)
    for g, (_, r) in enumerate(B_CONFIGS):
        store_folded(b_refs[g][2], matmul_block(NR_BLOCKS + 1 + g), r)


def _proj0(x, seq, norm_g, w, head_gain, gm, cos, sin):
    t_rows = x.shape[0]
    tm = TM
    n_pos_blocks = seq // tm
    row = lambda i: (i, 0)
    out_shapes = [jax.ShapeDtypeStruct((t_rows, A_Q_DIM), _bf16),
                  jax.ShapeDtypeStruct((t_rows, MXU_N), _bf16),
                  jax.ShapeDtypeStruct((t_rows, MXU_N), _bf16)]
    out_specs = [pl.BlockSpec((tm, A_Q_DIM), row), pl.BlockSpec((tm, MXU_N), row),
                 pl.BlockSpec((tm, MXU_N), row)]
    for _, r in B_CONFIGS:
        for _ in range(3):
            out_shapes.append(jax.ShapeDtypeStruct((t_rows // r, r * B_DIM), _bf16))
            out_specs.append(pl.BlockSpec((tm // r, r * B_DIM), row))
    return pl.pallas_call(
        _proj0_kernel,
        name="proj0",
        grid=(t_rows // tm,),
        in_specs=[pl.BlockSpec((tm, D_MODEL), row),
                  _const_spec((1, D_MODEL)),
                  _const_spec((D_MODEL, P0_COLS)),
                  _const_spec((1, NR_BLOCKS * MXU_N)),
                  _const_spec((MXU_N, MXU_N)),
                  pl.BlockSpec((tm, LANES), lambda i: (i % n_pos_blocks, 0)),
                  pl.BlockSpec((tm, LANES), lambda i: (i % n_pos_blocks, 0))],
        out_specs=out_specs,
        out_shape=out_shapes,
        scratch_shapes=[pltpu.VMEM((tm, D_MODEL), _bf16), pltpu.VMEM((2, tm, LANES), _f32)],
        compiler_params=_params(1),
    )(x, norm_g, w, head_gain, gm, cos, sin)


def _local_attn_kernel(sink_ref, q_ref, k_ref, v_ref, *refs, seq, tq, half, has_sink, want_lse):
    o_ref = refs[0]
    bias_scr = refs[-1]
    pair = pl.program_id(1)
    win = tq + 2 * half
    lane = lax.broadcasted_iota(jnp.int32, (1, LANES), 1)
    first_qk = ((lane // 32) % 2) == 0
    first_v = lane < HEAD_DIM
    row = lax.broadcasted_iota(jnp.int32, (2 * tq, 1), 0)
    top = row < tq
    qoff = jnp.where(top, row, row - tq)
    koff = lax.broadcasted_iota(jnp.int32, (1, win), 1)
    if has_sink:
        sink = jnp.where(top, sink_ref[pair, 0], sink_ref[pair, 1]) * LOG2_E
    for e in range(3):
        bias_scr[e] = jnp.where(jnp.abs(e * half + qoff - koff) <= half, 0.0, NEG)

    def step(g, carry):
        tiles = range(n_unroll)
        qs = [pl.multiple_of((g * n_unroll + u) * tq, tq) for u in tiles]
        ws = [pl.multiple_of(jnp.clip(qs[u] - half, 0, seq - win), half) for u in tiles]
        s = []
        for u in tiles:
            q2 = q_ref[0, pl.ds(qs[u], tq), :]
            zero = jnp.zeros_like(q2)
            qq = jnp.concatenate([jnp.where(first_qk, q2, zero), jnp.where(first_qk, zero, q2)], axis=0)
            kw = k_ref[0, pl.ds(ws[u], win), :]
            s.append(lax.dot_general(qq, kw, (((1,), (1,)), ((), ())), preferred_element_type=_f32))
        s = [s[u] + bias_scr[(qs[u] - ws[u]) // half] for u in tiles]
        m = [jnp.max(s[u], axis=-1, keepdims=True) for u in tiles]
        if has_sink:
            m = [jnp.maximum(m[u], sink) for u in tiles]
        p = [jnp.exp2(s[u] - m[u]) for u in tiles]
        den = [jnp.sum(p[u], axis=-1, keepdims=True) for u in tiles]
        if has_sink:
            den = [den[u] + jnp.exp2(sink - m[u]) for u in tiles]
        pv = [jnp.dot(p[u].astype(_bf16), v_ref[0, pl.ds(ws[u], win), :], preferred_element_type=_f32)
              for u in tiles]
        for u in tiles:
            od = pv[u] / den[u]
            o_ref[0, pl.ds(qs[u], tq), :] = jnp.where(first_v, od[:tq], od[tq:]).astype(o_ref.dtype)
            if want_lse:
                lse = m[u] + jnp.log2(den[u])
                refs[1][0, pl.ds(qs[u], tq), :] = jnp.where(first_v, lse[:tq], lse[tq:])
        return carry

    n_unroll = min(ATT_UNROLL, seq // tq)
    lax.fori_loop(0, seq // (tq * n_unroll), step, 0)


def _local_attn(q, k, v, sink, *, half, kv_of_pair, want_lse):
    bsz, seq, width = q.shape
    n_pairs = width // LANES
    tq = ATT_TQ
    assert seq % (tq * min(ATT_UNROLL, seq // tq)) == 0 and seq >= tq + 2 * half
    has_sink = sink is not None
    if not has_sink:
        sink = jnp.zeros((1, 2), _f32)
    qmap = lambda b, p: (b, 0, p)
    kmap = lambda b, p: (b, 0, kv_of_pair(p))
    out_shape = [jax.ShapeDtypeStruct((bsz, seq, width), _bf16)]
    out_specs = [pl.BlockSpec((1, seq, LANES), qmap)]
    if want_lse:
        out_shape.append(jax.ShapeDtypeStruct((bsz, seq, width), _f32))
        out_specs.append(pl.BlockSpec((1, seq, LANES), qmap))
    return pl.pallas_call(
        functools.partial(_local_attn_kernel, seq=seq, tq=tq, half=half, has_sink=has_sink,
                          want_lse=want_lse),
        name="local_attn_h%d" % half,
        grid=(bsz, n_pairs),
        in_specs=[pl.BlockSpec(memory_space=pltpu.SMEM),
                  pl.BlockSpec((1, seq, LANES), qmap),
                  pl.BlockSpec((1, seq, LANES), kmap),
                  pl.BlockSpec((1, seq, LANES), kmap)],
        out_specs=out_specs,
        out_shape=out_shape,
        scratch_shapes=[pltpu.VMEM((3, 2 * tq, tq + 2 * half), _f32)],
        compiler_params=_params(2),
    )(sink, q, k, v)


def _out0_kernel(x_ref, a_ref, o0_ref, l0_ref, o1_ref, l1_ref, o2_ref, l2_ref, wa_ref, wb_ref,
                 y_ref, o_scr, l_scr):
    tm = x_ref.shape[0]

    def unfold(src_ref, dst_scr, r):
        if r == 1:
            return src_ref[...].astype(_f32)
        for c in range(r):
            for t in range(2):
                lo = c * MXU_N + t * LANES
                dst_scr[t, pl.ds(c, tm // r, stride=r), :] = src_ref[:, lo:lo + LANES].astype(_f32)
        return jnp.concatenate([dst_scr[0], dst_scr[1]], axis=1)

    outs, lses = [], []
    for (o_ref, l_ref), (_, r) in zip(((o0_ref, l0_ref), (o1_ref, l1_ref), (o2_ref, l2_ref)), B_CONFIGS):
        outs.append(unfold(o_ref, o_scr, r))
        lses.append(unfold(l_ref, l_scr, r))
    m = jnp.maximum(jnp.maximum(lses[0], lses[1]), lses[2])
    es = [jnp.exp2(l - m) for l in lses]
    tot = es[0] + es[1] + es[2]
    b_out = (es[0] / tot) * outs[0] + (es[1] / tot) * outs[1] + (es[2] / tot) * outs[2]
    y = x_ref[...] + jnp.dot(a_ref[...], wa_ref[...], preferred_element_type=_f32)
    y_ref[...] = y + jnp.dot(b_out.astype(_bf16), wb_ref[...], preferred_element_type=_f32)


def _out0(x, a_out, b_outs, b_lses, wa, wb):
    t_rows = x.shape[0]
    tm = TM
    row = lambda i: (i, 0)
    in_specs = [pl.BlockSpec((tm, D_MODEL), row), pl.BlockSpec((tm, A_Q_DIM), row)]
    args = [x, a_out]
    for (_, r), o, l in zip(B_CONFIGS, b_outs, b_lses):
        in_specs += [pl.BlockSpec((tm // r, r * B_DIM), row)] * 2
        args += [o, l]
    in_specs += [_const_spec((A_Q_DIM, D_MODEL)), _const_spec((B_DIM, D_MODEL))]
    args += [wa, wb]
    return pl.pallas_call(
        _out0_kernel,
        name="out0",
        grid=(t_rows // tm,),
        in_specs=in_specs,
        out_specs=pl.BlockSpec((tm, D_MODEL), row),
        out_shape=jax.ShapeDtypeStruct((t_rows, D_MODEL), _f32),
        scratch_shapes=[pltpu.VMEM((2, tm, LANES), _f32), pltpu.VMEM((2, tm, LANES), _f32)],
        compiler_params=_params(1),
    )(*args)


def _ffn_kernel(xp_ref, x_ref, xn_ref, g_ref, wg_ref, wv_ref, cw_ref, wd_ref, y_ref,
                h_scr, gate_a, gate_b, val_a, val_b, acc_scr, *, tiles_per_seq):
    tm = x_ref.shape[0]
    i = pl.program_id(0)
    pos = i % tiles_per_seq
    keep_prev = jnp.where(pos == 0, 0.0, 1.0)
    keep_next = jnp.where(pos == tiles_per_seq - 1, 0.0, 1.0)
    g = g_ref[...]
    h_scr[0:HALO, :] = _rms(xp_ref[...], g).astype(_bf16)
    h_scr[HALO:HALO + tm, :] = _rms(x_ref[...], g).astype(_bf16)
    h_scr[HALO + tm:, :] = _rms(xn_ref[...], g).astype(_bf16)

    def up(j, gate_scr, val_scr):
        gate_scr[...] = jnp.dot(h_scr[...], wg_ref[j], preferred_element_type=_f32)
        gate_scr[HALO - 8:HALO, :] = gate_scr[HALO - 8:HALO, :] * keep_prev
        gate_scr[HALO + tm:HALO + tm + 8, :] = gate_scr[HALO + tm:HALO + tm + 8, :] * keep_next
        val_scr[...] = jnp.dot(h_scr[HALO:HALO + tm, :], wv_ref[j], preferred_element_type=_f32)

    def act(j, gate_scr, val_scr):
        cw = cw_ref[j]
        conv = (cw[0:1] * gate_scr[HALO - 1:HALO - 1 + tm, :]
                + cw[1:2] * gate_scr[HALO:HALO + tm, :]
                + cw[2:3] * gate_scr[HALO + 1:HALO + 1 + tm, :]
                + cw[3:4])
        return (jax.nn.gelu(conv) * val_scr[...]).astype(_bf16)

    bufs = ((gate_a, val_a), (gate_b, val_b))

    def run_chunks(j0, count, up_following):
        total = None
        for u in range(count):
            if u < count - 1 or up_following:
                up(j0 + u + 1, *bufs[(u + 1) % 2])
            a = act(j0 + u, *bufs[u % 2])
            d = jnp.dot(a, wd_ref[j0 + u], preferred_element_type=_f32)
            total = d if total is None else total + d
        return total

    def step(jj, carry):
        acc_scr[...] += run_chunks(FF_UNROLL * jj, FF_UNROLL, True)
        return carry

    n_loop = (N_FF_CHUNK - 1) // FF_UNROLL
    acc_scr[...] = jnp.zeros_like(acc_scr)
    up(0, gate_a, val_a)
    lax.fori_loop(0, n_loop, step, 0)
    tail = run_chunks(n_loop * FF_UNROLL, N_FF_CHUNK - n_loop * FF_UNROLL, False)
    y_ref[...] = x_ref[...] + (acc_scr[...] + tail)


def _ffn(x, seq, norm_g, wg, wv, cw, wd):
    t_rows = x.shape[0]
    tm = TM
    per = tm // HALO
    n_halo_blocks = t_rows // HALO
    return pl.pallas_call(
        functools.partial(_ffn_kernel, tiles_per_seq=seq // tm),
        name="conv_ffn",
        grid=(t_rows // tm,),
        in_specs=[pl.BlockSpec((HALO, D_MODEL), lambda i: (jnp.maximum(i * per - 1, 0), 0)),
                  pl.BlockSpec((tm, D_MODEL), lambda i: (i, 0)),
                  pl.BlockSpec((HALO, D_MODEL), lambda i: (jnp.minimum((i + 1) * per, n_halo_blocks - 1), 0)),
                  _const_spec((1, D_MODEL)),
                  _const_spec((N_FF_CHUNK, D_MODEL, FF_CHUNK)),
                  _const_spec((N_FF_CHUNK, D_MODEL, FF_CHUNK)),
                  _const_spec((N_FF_CHUNK, 8, FF_CHUNK)),
                  _const_spec((N_FF_CHUNK, FF_CHUNK, D_MODEL))],
        out_specs=pl.BlockSpec((tm, D_MODEL), lambda i: (i, 0)),
        out_shape=jax.ShapeDtypeStruct((t_rows, D_MODEL), _f32),
        scratch_shapes=[pltpu.VMEM((tm + 2 * HALO, D_MODEL), _bf16),
                        pltpu.VMEM((tm + 2 * HALO, FF_CHUNK), _f32),
                        pltpu.VMEM((tm + 2 * HALO, FF_CHUNK), _f32),
                        pltpu.VMEM((tm, FF_CHUNK), _f32),
                        pltpu.VMEM((tm, FF_CHUNK), _f32),
                        pltpu.VMEM((tm, D_MODEL), _f32)],
        compiler_params=_params(1),
    )(x, x, x, norm_g, wg, wv, cw, wd)


def _proj1_kernel(x_ref, g_ref, win_ref, qlg_ref, kvg_ref, wq_ref, wk_ref, wvt_ref, qg_ref, kg_ref,
                  gm_ref, cos_ref, sin_ref, q_ref, k_ref, vt_ref):
    tm = x_ref.shape[0]
    h = _rms(x_ref[...], g_ref[...]).astype(_bf16)
    proj = jnp.dot(h, win_ref[...], preferred_element_type=_f32)
    cq = _rms(proj[:, :C_Q_LORA], qlg_ref[...]).astype(_bf16)
    ckv_f32 = _rms(proj[:, C_Q_LORA:C_Q_LORA + C_KV_LORA], kvg_ref[...])
    ckv = ckv_f32.astype(_bf16)
    ckv_t = ckv_f32.T.astype(_bf16)
    k_rope = proj[:, C_Q_LORA + C_KV_LORA:]
    k_rope2 = jnp.concatenate([k_rope, k_rope], axis=1)
    cos = cos_ref[...]
    sin = sin_ref[...]
    gm = gm_ref[...]
    qg = jnp.concatenate([qg_ref[...]] * 2, axis=1)
    kg = jnp.concatenate([kg_ref[...]] * 2, axis=1)
    for b in range(C_HEADS // 2):
        cols = slice(b * MXU_N, (b + 1) * MXU_N)
        aq = jnp.dot(cq, wq_ref[:, cols], preferred_element_type=_f32)
        lo, hi = _norm_rope(aq, gm, qg, cos, sin, 1.0 / C_QK)
        q_ref[:, cols] = jnp.concatenate([lo, hi], axis=1).astype(_bf16)
        ak = jnp.dot(ckv, wk_ref[:, cols], preferred_element_type=_f32) + k_rope2
        lo, hi = _norm_rope(ak, gm, kg, cos, sin, 1.0 / C_QK)
        k_ref[:, cols] = jnp.concatenate([lo, hi], axis=1).astype(_bf16)
    r = lax.broadcasted_iota(jnp.int32, (MXU_N, tm), 0)
    for b in range(C_HEADS * C_VROWS // MXU_N):
        rows = slice(b * MXU_N, (b + 1) * MXU_N)
        vt = jnp.dot(wvt_ref[rows, :], ckv_t, preferred_element_type=_f32)
        ones_rows = (r + b * MXU_N) % C_VROWS >= C_V
        vt_ref[0, rows, :] = jnp.where(ones_rows, 1.0, vt).astype(_bf16)


def _proj1(x, seq, norm_g, win, qlg, kvg, wq, wk, wvt, qg, kg, gm, cos, sin):
    t_rows = x.shape[0]
    tm = TM
    n_pos_blocks = seq // tm
    row = lambda i: (i, 0)
    qk_w = C_HEADS * LANES
    v_w = C_HEADS * C_VROWS
    return pl.pallas_call(
        _proj1_kernel,
        name="proj1",
        grid=(t_rows // tm,),
        in_specs=[pl.BlockSpec((tm, D_MODEL), row),
                  _const_spec((1, D_MODEL)),
                  _const_spec(win.shape),
                  _const_spec((1, C_Q_LORA)), _const_spec((1, C_KV_LORA)),
                  _const_spec((C_Q_LORA, qk_w)), _const_spec((C_KV_LORA, qk_w)),
                  _const_spec((v_w, C_KV_LORA)),
                  _const_spec((1, LANES)), _const_spec((1, LANES)),
                  _const_spec((MXU_N, MXU_N)),
                  pl.BlockSpec((tm, LANES), lambda i: (i % n_pos_blocks, 0)),
                  pl.BlockSpec((tm, LANES), lambda i: (i % n_pos_blocks, 0))],
        out_specs=[pl.BlockSpec((tm, qk_w), row), pl.BlockSpec((tm, qk_w), row),
                   pl.BlockSpec((1, v_w, tm), lambda i: (i, 0, 0))],
        out_shape=[jax.ShapeDtypeStruct((t_rows, qk_w), _bf16),
                   jax.ShapeDtypeStruct((t_rows, qk_w), _bf16),
                   jax.ShapeDtypeStruct((t_rows // tm, v_w, tm), _bf16)],
        compiler_params=_params(1),
    )(x, norm_g, win, qlg, kvg, wq, wk, wvt, qg, kg, gm, cos, sin)


def _mla_kernel(q_ref, k_ref, vt_ref, o_ref, sa_ref, sb_ref, *, seq, tk, unroll, interleave, skew):
    tq = q_ref.shape[1]
    n_tiles = seq // tk
    s_bufs = ((sa_ref, 0), (sb_ref, skew))

    def scores(t, buf, hh):
        ks = pl.multiple_of(t * tk, tk)
        kt = k_ref[0, pl.ds(ks, tk), hh * LANES:(hh + 1) * LANES]
        qh = q_ref[0, :, hh * LANES:(hh + 1) * LANES]
        s = lax.dot_general(kt, qh, (((1,), (1,)), ((), ())), preferred_element_type=_f32)
        ref, row0 = buf
        ref[hh, row0:row0 + tk, :] = s
        return jnp.max(s, axis=0, keepdims=True)

    def consume(t, buf, hh, tmax, state):
        m_old, acc = state
        m_new = jnp.maximum(m_old, tmax)
        alpha = jnp.exp2(m_old - m_new)
        ref, row0 = buf
        p = jnp.exp2(ref[hh, row0:row0 + tk, :] - m_new).astype(_bf16)
        vt = vt_ref[t, hh * C_VROWS:(hh + 1) * C_VROWS, :]
        acc = alpha * acc + jnp.dot(vt, p, preferred_element_type=_f32)
        return m_new, acc

    def run_tiles(t0, tmax, state, score_following):
        tmax, state = list(tmax), list(state)
        for u in range(unroll):
            score_next = u < unroll - 1 or score_following
            cur, nxt = s_bufs[u % 2], s_bufs[(u + 1) % 2]
            if interleave:
                for hh in range(2):
                    if score_next:
                        tmax_next = scores(t0 + u + 1, nxt, hh)
                    state[hh] = consume(t0 + u, cur, hh, tmax[hh], state[hh])
                    if score_next:
                        tmax[hh] = tmax_next
            else:
                if score_next:
                    tmax_next = [scores(t0 + u + 1, nxt, hh) for hh in range(2)]
                state = [consume(t0 + u, cur, hh, tmax[hh], state[hh]) for hh in range(2)]
                if score_next:
                    tmax = tmax_next
        return tuple(tmax), tuple(state)

    state = tuple((jnp.full((1, tq), -jnp.inf, _f32), jnp.zeros((C_VROWS, tq), _f32)) for _ in range(2))
    tmax = tuple(scores(0, s_bufs[0], hh) for hh in range(2))
    tmax, state = lax.fori_loop(
        0, n_tiles // unroll - 1,
        lambda jj, c: run_tiles(unroll * jj, c[0], c[1], True), (tmax, state))
    _, ((_, acc0), (_, acc1)) = run_tiles(n_tiles - unroll, tmax, state, False)
    out_t = jnp.concatenate([acc0[:C_V] / acc0[C_V:C_V + 1], acc1[:C_V] / acc1[C_V:C_V + 1]], axis=0)
    o_ref[0] = out_t.T.astype(o_ref.dtype)


def _mla_attn(q, k, vt):
    bsz, seq, _ = q.shape
    n_pairs = C_HEADS // 2
    tq, tk = MLA_TQ, MLA_TK
    n_tiles = seq // tk
    long_seq = n_tiles >= 3 * MLA_UNROLL
    unroll = MLA_UNROLL if long_seq else 2
    interleave = not long_seq
    skew = 0 if long_seq else 8
    assert vt.shape[2] == tk and n_tiles % unroll == 0 and seq % tq == 0
    return pl.pallas_call(
        functools.partial(_mla_kernel, seq=seq, tk=tk, unroll=unroll, interleave=interleave, skew=skew),
        name="mla_attn",
        grid=(bsz, n_pairs, seq // tq),
        in_specs=[pl.BlockSpec((1, tq, 2 * LANES), lambda b, p, i: (b, i, p)),
                  pl.BlockSpec((1, seq, 2 * LANES), lambda b, p, i: (b, 0, p)),
                  pl.BlockSpec((seq // tk, 2 * C_VROWS, tk), lambda b, p, i: (b, p, 0))],
        out_specs=pl.BlockSpec((1, tq, LANES), lambda b, p, i: (b, i, p)),
        out_shape=jax.ShapeDtypeStruct((bsz, seq, C_HEADS * C_V), _bf16),
        scratch_shapes=[pltpu.VMEM((2, tk, tq), _f32), pltpu.VMEM((2, tk + skew, tq), _f32)],
        compiler_params=_params(3),
    )(q, k, vt)


def _out1_kernel(x_ref, o_ref, w_ref, y_ref):
    y_ref[...] = x_ref[...] + jnp.dot(o_ref[...], w_ref[...], preferred_element_type=_f32)


def _out1(x, o, w):
    t_rows = x.shape[0]
    tm = TM
    row = lambda i: (i, 0)
    return pl.pallas_call(
        _out1_kernel,
        name="out1",
        grid=(t_rows // tm,),
        in_specs=[pl.BlockSpec((tm, D_MODEL), row), pl.BlockSpec((tm, o.shape[1]), row),
                  _const_spec(w.shape)],
        out_specs=pl.BlockSpec((tm, D_MODEL), row),
        out_shape=jax.ShapeDtypeStruct((t_rows, D_MODEL), _f32),
        compiler_params=_params(1),
    )(x, o, w)


def _pair_cols(a, b):
    r = np.arange(HEAD_DIM // 2)
    return np.concatenate([a + r, b + r, a + HEAD_DIM // 2 + r, b + HEAD_DIM // 2 + r])


_PAIR_DIMS = _pair_cols(0, 0)


def _layer0_columns():
    nr, val = [], []
    for p in range(A_Q_HEADS // 2):
        nr.append(_pair_cols(2 * p * HEAD_DIM, (2 * p + 1) * HEAD_DIM))
    for h in range(A_KV_HEADS):
        nr.append(_pair_cols(A_Q_DIM + h * HEAD_DIM, A_Q_DIM + h * HEAD_DIM))
    b0 = A_Q_DIM + 2 * A_KV_DIM
    for g in range(N_BRANCH):
        for t in range(2):
            base = b0 + g * 3 * B_DIM + t * B_DIM
            for p in range(B_HEADS // 2):
                nr.append(_pair_cols(base + 2 * p * HEAD_DIM, base + (2 * p + 1) * HEAD_DIM))
    d = np.arange(HEAD_DIM)
    for h in range(A_KV_HEADS):
        base = A_Q_DIM + A_KV_DIM + h * HEAD_DIM
        val.append(np.concatenate([base + d, base + d]))
    for g in range(N_BRANCH):
        val.append(b0 + g * 3 * B_DIM + 2 * B_DIM + np.arange(B_DIM))
    return np.concatenate(nr + val)


def _layer0_gains(a_q_gain, a_k_gain, b_q_gain, b_k_gain):
    scale = (HEAD_DIM ** -0.5) * LOG2_E
    aq = a_q_gain[_PAIR_DIMS] * scale
    ak = a_k_gain[_PAIR_DIMS]
    parts = [aq] * (A_Q_HEADS // 2) + [ak] * A_KV_HEADS
    for g in range(N_BRANCH):
        parts += [b_q_gain[g][_PAIR_DIMS] * scale] * (B_HEADS // 2)
        parts += [b_k_gain[g][_PAIR_DIMS]] * (B_HEADS // 2)
    return jnp.concatenate(parts)[None, :]


def _layer0_group_matrix():
    lane = np.arange(MXU_N)
    head = (lane // LANES) * 2 + (lane // 32) % 2
    return jnp.asarray(head[:, None] == head[None, :], _bf16)


def _rope_tables(seq, dim):
    inv = jnp.power(ROPE_THETA, -jnp.arange(0, dim, 2, dtype=_f32) / dim)
    ang = jnp.arange(seq, dtype=_f32)[:, None] * inv[None, :]
    return jnp.cos(ang), jnp.sin(ang)


def _layer0_rope(seq):
    cos, sin = _rope_tables(seq, HEAD_DIM)
    return (jnp.concatenate([cos, cos, cos, cos], axis=1),
            jnp.concatenate([-sin, -sin, sin, sin], axis=1))


_C_HALF = C_ROPE // 2
_C_LANE_DIM = np.full(LANES, -1)
_C_LANE_DIM[0:_C_HALF] = C_NOPE + np.arange(_C_HALF)
_C_LANE_DIM[_C_HALF:_C_HALF + 32] = np.arange(32)
_C_LANE_DIM[64:64 + _C_HALF] = C_NOPE + _C_HALF + np.arange(_C_HALF)
_C_LANE_DIM[64 + _C_HALF:64 + _C_HALF + 32] = 32 + np.arange(32)


def _place_heads(w, per_head, lane_dim):
    idx = (np.arange(C_HEADS)[:, None] * per_head + np.maximum(lane_dim, 0)[None, :]).reshape(-1)
    keep = np.tile(lane_dim >= 0, C_HEADS)
    return jnp.where(jnp.asarray(keep)[None, :], w[:, idx], 0.0)


def _layer1_rope(seq):
    cos, sin = _rope_tables(seq, C_ROPE)
    ones = jnp.ones((seq, 64 - _C_HALF), _f32)
    zeros = jnp.zeros((seq, 64 - _C_HALF), _f32)
    return (jnp.concatenate([cos, ones, cos, ones], axis=1),
            jnp.concatenate([-sin, zeros, sin, zeros], axis=1))


def _layer1_group_matrix():
    lane = np.arange(MXU_N)
    return jnp.asarray((lane[:, None] // LANES) == (lane[None, :] // LANES), _bf16)


def _trunk(x3, e_norm, e_w_in, e_a_q_gain, e_a_k_gain, e_a_sink, e_b_q_gain, e_b_k_gain, e_w_out,
           o_norm, o_w_in, o_q_lora_gain, o_w_uq, o_kv_gain, o_w_ukv, o_q_gain, o_k_gain, o_w_out,
           f_norm, f_w_up, f_conv_w, f_conv_b, f_w_down):
    bsz, seq, _ = x3.shape
    assert seq % TM == 0
    x = x3.reshape(bsz * seq, D_MODEL)
    for layer in range(DEPTH):
        i = layer // 2
        if layer % 2 == 0:
            w0 = e_w_in[i][:, _layer0_columns()].astype(_bf16)
            cos, sin = _layer0_rope(seq)
            outs = _proj0(x, seq, e_norm[i][None, :], w0,
                          _layer0_gains(e_a_q_gain[i], e_a_k_gain[i], e_b_q_gain[i], e_b_k_gain[i]),
                          _layer0_group_matrix(), cos, sin)
            qa, kd, vd = outs[:3]
            a_out = _local_attn(qa.reshape(bsz, seq, A_Q_DIM), kd.reshape(bsz, seq, MXU_N),
                                vd.reshape(bsz, seq, MXU_N), e_a_sink[i].reshape(A_Q_HEADS // 2, 2),
                                half=A_HALF_WINDOW, kv_of_pair=lambda p: p // 2, want_lse=False)[0]
            b_outs, b_lses = [], []
            for g, (window, r) in enumerate(B_CONFIGS):
                bq, bk, bv = (t.reshape(bsz, seq // r, r * B_DIM) for t in outs[3 + 3 * g:6 + 3 * g])
                o, lse = _local_attn(bq, bk, bv, None, half=(window // 2) // r,
                                     kv_of_pair=lambda p: p, want_lse=True)
                b_outs.append(o.reshape(bsz * seq // r, r * B_DIM))
                b_lses.append(lse.reshape(bsz * seq // r, r * B_DIM))
            w_out = e_w_out[i].astype(_bf16)
            x = _out0(x, a_out.reshape(bsz * seq, A_Q_DIM), b_outs, b_lses,
                      w_out[:A_Q_DIM], w_out[A_Q_DIM:])
        else:
            w_in = o_w_in[i]
            k_rope_cols = jnp.zeros((D_MODEL, LANES), _f32)
            k_rope_cols = k_rope_cols.at[:, 0:_C_HALF].set(w_in[:, C_Q_LORA + C_KV_LORA:C_Q_LORA + C_KV_LORA + _C_HALF])
            k_rope_cols = k_rope_cols.at[:, 64:64 + _C_HALF].set(w_in[:, C_Q_LORA + C_KV_LORA + _C_HALF:])
            win = jnp.concatenate([w_in[:, :C_Q_LORA + C_KV_LORA], k_rope_cols], axis=1).astype(_bf16)
            wq = _place_heads(o_w_uq[i], C_QK, _C_LANE_DIM).astype(_bf16)
            nope_only = np.where(_C_LANE_DIM < C_NOPE, _C_LANE_DIM, -1)
            wk = _place_heads(o_w_ukv[i], C_NOPE + C_V, nope_only).astype(_bf16)
            r_in_head = np.arange(C_HEADS * C_VROWS) % C_VROWS
            head = np.arange(C_HEADS * C_VROWS) // C_VROWS
            v_col = head * (C_NOPE + C_V) + C_NOPE + np.minimum(r_in_head, C_V - 1)
            wvt = jnp.where(jnp.asarray(r_in_head < C_V)[:, None], o_w_ukv[i][:, v_col].T, 0.0).astype(_bf16)
            lane_ok = jnp.asarray(_C_LANE_DIM >= 0)
            q_scale = (C_QK ** -0.5) * LOG2_E
            qg = jnp.where(lane_ok, o_q_gain[i][np.maximum(_C_LANE_DIM, 0)] * q_scale, 0.0)[None, :]
            kg = jnp.where(lane_ok, o_k_gain[i][np.maximum(_C_LANE_DIM, 0)], 0.0)[None, :]
            cos, sin = _layer1_rope(seq)
            q, k, vt = _proj1(x, seq, o_norm[i][None, :], win, o_q_lora_gain[i][None, :],
                              o_kv_gain[i][None, :], wq, wk, wvt, qg, kg, _layer1_group_matrix(),
                              cos, sin)
            o = _mla_attn(q.reshape(bsz, seq, -1), k.reshape(bsz, seq, -1), vt)
            x = _out1(x, o.reshape(bsz * seq, C_HEADS * C_V), o_w_out[i].astype(_bf16))
        w_up = f_w_up[layer]
        wg = w_up[:, :D_FF].reshape(D_MODEL, N_FF_CHUNK, FF_CHUNK).transpose(1, 0, 2).astype(_bf16)
        wv_ = w_up[:, D_FF:].reshape(D_MODEL, N_FF_CHUNK, FF_CHUNK).transpose(1, 0, 2).astype(_bf16)
        cw = jnp.concatenate([f_conv_w[layer], f_conv_b[layer][None, :], jnp.zeros((4, D_FF), _f32)], axis=0)
        cw = cw.reshape(8, N_FF_CHUNK, FF_CHUNK).transpose(1, 0, 2)
        wd = f_w_down[layer].reshape(N_FF_CHUNK, FF_CHUNK, D_MODEL).astype(_bf16)
        x = _ffn(x, seq, f_norm[layer][None, :], wg, wv_, cw, wd)
    return x.reshape(bsz, seq, D_MODEL)


def kernel(x_prompt, x_sample, e_norm, e_w_in, e_a_q_gain, e_a_k_gain, e_a_sink, e_b_q_gain, e_b_k_gain, e_w_out, o_norm, o_w_in, o_q_lora_gain, o_w_uq, o_kv_gain, o_w_ukv, o_q_gain, o_k_gain, o_w_out, f_norm, f_w_up, f_conv_w, f_conv_b, f_w_down):
    weights = (e_norm, e_w_in, e_a_q_gain, e_a_k_gain, e_a_sink, e_b_q_gain, e_b_k_gain, e_w_out,
               o_norm, o_w_in, o_q_lora_gain, o_w_uq, o_kv_gain, o_w_ukv, o_q_gain, o_k_gain, o_w_out,
               f_norm, f_w_up, f_conv_w, f_conv_b, f_w_down)
    return (_trunk(x_prompt, *weights), _trunk(x_sample, *weights))
```

```python
import functools

import numpy as np
import jax
import jax.numpy as jnp
from jax import lax
from jax.experimental import pallas as pl
from jax.experimental.pallas import tpu as pltpu

D_MODEL = 1024
HEAD_DIM = 64
ROPE_THETA = 10000.0
EPS = 1e-6
NEG = -1e30
A_Q_HEADS = 8
A_KV_HEADS = 2
A_HALF_WINDOW = 128
B_HEADS = 4
B_CONFIGS = ((128, 1), (512, 4), (2048, 16))
N_BRANCH = len(B_CONFIGS)
A_Q_DIM = A_Q_HEADS * HEAD_DIM
A_KV_DIM = A_KV_HEADS * HEAD_DIM
B_DIM = B_HEADS * HEAD_DIM
C_HEADS = 16
C_NOPE = 64
C_ROPE = 32
C_QK = C_NOPE + C_ROPE
C_V = 64
C_Q_LORA = 256
C_KV_LORA = 256
D_FF = 2816
DEPTH = 2

LANES = 128
MXU_N = 256
BF16_ROWS = 16
VMEM_LIMIT = 56 * 1024 * 1024
C_VROWS = C_V + BF16_ROWS

TM = 512
ATT_TQ = 128
ATT_UNROLL = 4
ATT_STEP_BYTES = 1 << 20
ATT_MAX_GROUP = 4
MLA_TQ = 512
MLA_TK = 512
MLA_UNROLL = 4
FF_CHUNK = 256
N_FF_CHUNK = D_FF // FF_CHUNK
FF_UNROLL = 4
HALO = BF16_ROWS

NR_BLOCKS = 9
P0_COLS = 13 * MXU_N

LOG2_E = float(np.log2(np.e))

_f32 = jnp.float32
_bf16 = jnp.bfloat16


def _const_spec(shape):
    nd = len(shape)
    return pl.BlockSpec(shape, lambda *_: (0,) * nd, pipeline_mode=pl.Buffered(1))


def _params(n_axes):
    return pltpu.CompilerParams(dimension_semantics=("arbitrary",) * n_axes,
                                vmem_limit_bytes=VMEM_LIMIT)


def _rms(x, g):
    y = x * lax.rsqrt(jnp.mean(x * x, axis=-1, keepdims=True) + EPS)
    return y * g


def _group_sumsq(a, gm):
    sq = a * a
    hi = sq.astype(_bf16)
    lo = (sq - hi.astype(_f32)).astype(_bf16)
    return (jnp.dot(hi, gm, preferred_element_type=_f32)
            + jnp.dot(lo, gm, preferred_element_type=_f32))


def _norm_rope(a, gm, gain, cos, sin, inv_dim):
    ss = _group_sumsq(a, gm)
    y = a * lax.rsqrt(ss * inv_dim + EPS) * gain
    halves = []
    for t in range(2):
        yt = y[:, t * LANES:(t + 1) * LANES]
        halves.append(yt * cos + pltpu.roll(yt, LANES // 2, 1) * sin)
    return halves


def _proj0_kernel(x_ref, g_ref, w_ref, hg_ref, gm_ref, cos_ref, sin_ref,
                  qa_ref, kd_ref, vd_ref,
                  bq0_ref, bk0_ref, bv0_ref, bq1_ref, bk1_ref, bv1_ref, bq2_ref, bk2_ref, bv2_ref,
                  h_scr, fold_scr):
    tm = x_ref.shape[0]
    h_scr[...] = _rms(x_ref[...], g_ref[...]).astype(_bf16)
    cos = cos_ref[...]
    sin = sin_ref[...]
    gm = gm_ref[...]

    blocks = {}

    def matmul_block(j):
        if j not in blocks:
            n_blk = min(2, P0_COLS // MXU_N - j)
            a = jnp.dot(h_scr[...], w_ref[:, j * MXU_N:(j + n_blk) * MXU_N], preferred_element_type=_f32)
            for t in range(n_blk):
                blocks[j + t] = a[:, t * MXU_N:(t + 1) * MXU_N]
        return blocks.pop(j)

    def store_folded(out_ref, val, r):
        if r == 1:
            out_ref[...] = val.astype(_bf16)
            return
        for t in range(2):
            fold_scr[t] = val[:, t * LANES:(t + 1) * LANES]
        for c in range(r):
            for t in range(2):
                lo = c * MXU_N + t * LANES
                out_ref[:, lo:lo + LANES] = fold_scr[t, pl.ds(c, tm // r, stride=r), :].astype(_bf16)

    def nr_block(j):
        lo, hi = _norm_rope(matmul_block(j), gm, hg_ref[:, j * MXU_N:(j + 1) * MXU_N],
                            cos, sin, 1.0 / HEAD_DIM)
        return jnp.concatenate([lo, hi], axis=1)

    qa_ref[:, 0:MXU_N] = nr_block(0).astype(_bf16)
    qa_ref[:, MXU_N:2 * MXU_N] = nr_block(1).astype(_bf16)
    kd_ref[...] = nr_block(2).astype(_bf16)
    b_refs = ((bq0_ref, bk0_ref, bv0_ref), (bq1_ref, bk1_ref, bv1_ref), (bq2_ref, bk2_ref, bv2_ref))
    for g, (_, r) in enumerate(B_CONFIGS):
        store_folded(b_refs[g][0], nr_block(3 + 2 * g), r)
        store_folded(b_refs[g][1], nr_block(4 + 2 * g), r)
    vd_ref[...] = matmul_block(NR_BLOCKS).astype(_bf16)
    for g, (_, r) in enumerate(B_CONFIGS):
        store_folded(b_refs[g][2], matmul_block(NR_BLOCKS + 1 + g), r)


def _proj0(x, seq, norm_g, w, head_gain, gm, cos, sin):
    t_rows = x.shape[0]
    tm = TM
    n_pos_blocks = seq // tm
    row = lambda i: (i, 0)
    out_shapes = [jax.ShapeDtypeStruct((t_rows, A_Q_DIM), _bf16),
                  jax.ShapeDtypeStruct((t_rows, MXU_N), _bf16),
                  jax.ShapeDtypeStruct((t_rows, MXU_N), _bf16)]
    out_specs = [pl.BlockSpec((tm, A_Q_DIM), row), pl.BlockSpec((tm, MXU_N), row),
                 pl.BlockSpec((tm, MXU_N), row)]
    for _, r in B_CONFIGS:
        for _ in range(3):
            out_shapes.append(jax.ShapeDtypeStruct((t_rows // r, r * B_DIM), _bf16))
            out_specs.append(pl.BlockSpec((tm // r, r * B_DIM), row))
    return pl.pallas_call(
        _proj0_kernel,
        name="proj0",
        grid=(t_rows // tm,),
        in_specs=[pl.BlockSpec((tm, D_MODEL), row),
                  _const_spec((1, D_MODEL)),
                  _const_spec((D_MODEL, P0_COLS)),
                  _const_spec((1, NR_BLOCKS * MXU_N)),
                  _const_spec((MXU_N, MXU_N)),
                  pl.BlockSpec((tm, LANES), lambda i: (i % n_pos_blocks, 0)),
                  pl.BlockSpec((tm, LANES), lambda i: (i % n_pos_blocks, 0))],
        out_specs=out_specs,
        out_shape=out_shapes,
        scratch_shapes=[pltpu.VMEM((tm, D_MODEL), _bf16), pltpu.VMEM((2, tm, LANES), _f32)],
        compiler_params=_params(1),
    )(x, norm_g, w, head_gain, gm, cos, sin)


def _local_attn_kernel(sink_ref, q_ref, k_ref, v_ref, *refs, seq, tq, half, group, has_sink, want_lse):
    o_ref = refs[0]
    bias_scr = refs[-1]
    win = tq + 2 * half
    lane = lax.broadcasted_iota(jnp.int32, (1, LANES), 1)
    first_qk = ((lane // 32) % 2) == 0
    first_v = lane < HEAD_DIM
    row = lax.broadcasted_iota(jnp.int32, (2 * tq, 1), 0)
    top = row < tq
    qoff = jnp.where(top, row, row - tq)
    koff = lax.broadcasted_iota(jnp.int32, (1, win), 1)
    for e in range(3):
        bias_scr[e] = jnp.where(jnp.abs(e * half + qoff - koff) <= half, 0.0, NEG)

    def step(g, carry, lanes, sink):
        tiles = range(n_unroll)
        qs = [pl.multiple_of((g * n_unroll + u) * tq, tq) for u in tiles]
        ws = [pl.multiple_of(jnp.clip(qs[u] - half, 0, seq - win), half) for u in tiles]
        s = []
        for u in tiles:
            q2 = q_ref[0, pl.ds(qs[u], tq), lanes]
            zero = jnp.zeros_like(q2)
            qq = jnp.concatenate([jnp.where(first_qk, q2, zero), jnp.where(first_qk, zero, q2)], axis=0)
            kw = k_ref[0, pl.ds(ws[u], win), lanes]
            s.append(lax.dot_general(qq, kw, (((1,), (1,)), ((), ())), preferred_element_type=_f32))
        s = [s[u] + bias_scr[(qs[u] - ws[u]) // half] for u in tiles]
        m = [jnp.max(s[u], axis=-1, keepdims=True) for u in tiles]
        if has_sink:
            m = [jnp.maximum(m[u], sink) for u in tiles]
        p = [jnp.exp2(s[u] - m[u]) for u in tiles]
        den = [jnp.sum(p[u], axis=-1, keepdims=True) for u in tiles]
        if has_sink:
            den = [den[u] + jnp.exp2(sink - m[u]) for u in tiles]
        pv = [jnp.dot(p[u].astype(_bf16), v_ref[0, pl.ds(ws[u], win), lanes], preferred_element_type=_f32)
              for u in tiles]
        for u in tiles:
            od = pv[u] / den[u]
            o_ref[0, pl.ds(qs[u], tq), lanes] = jnp.where(first_v, od[:tq], od[tq:]).astype(o_ref.dtype)
            if want_lse:
                lse = m[u] + jnp.log2(den[u])
                refs[1][0, pl.ds(qs[u], tq), lanes] = jnp.where(first_v, lse[:tq], lse[tq:])
        return carry

    n_unroll = min(ATT_UNROLL, seq // tq)
    for pp in range(group):
        sink = None
        if has_sink:
            pair = pl.program_id(1) * group + pp
            sink = jnp.where(top, sink_ref[pair, 0], sink_ref[pair, 1]) * LOG2_E
        lax.fori_loop(0, seq // (tq * n_unroll),
                      functools.partial(step, lanes=slice(pp * LANES, (pp + 1) * LANES), sink=sink), 0)


def _local_attn(q, k, v, sink, *, half, kv_of_pair, want_lse):
    bsz, seq, width = q.shape
    n_pairs = width // LANES
    tq = ATT_TQ
    group = 1
    if kv_of_pair is None:
        kv_of_pair = lambda p: p
        while group < ATT_MAX_GROUP and n_pairs % (2 * group) == 0 and 2 * group * seq * LANES * 2 <= ATT_STEP_BYTES:
            group *= 2
    blk = group * LANES
    assert seq % (tq * min(ATT_UNROLL, seq // tq)) == 0 and seq >= tq + 2 * half
    has_sink = sink is not None
    if not has_sink:
        sink = jnp.zeros((1, 2), _f32)
    qmap = lambda b, p: (b, 0, p)
    kmap = lambda b, p: (b, 0, kv_of_pair(p))
    out_shape = [jax.ShapeDtypeStruct((bsz, seq, width), _bf16)]
    out_specs = [pl.BlockSpec((1, seq, blk), qmap)]
    if want_lse:
        out_shape.append(jax.ShapeDtypeStruct((bsz, seq, width), _f32))
        out_specs.append(pl.BlockSpec((1, seq, blk), qmap))
    return pl.pallas_call(
        functools.partial(_local_attn_kernel, seq=seq, tq=tq, half=half, group=group, has_sink=has_sink,
                          want_lse=want_lse),
        name="local_attn_h%d" % half,
        grid=(bsz, n_pairs // group),
        in_specs=[pl.BlockSpec(memory_space=pltpu.SMEM),
                  pl.BlockSpec((1, seq, blk), qmap),
                  pl.BlockSpec((1, seq, blk), kmap),
                  pl.BlockSpec((1, seq, blk), kmap)],
        out_specs=out_specs,
        out_shape=out_shape,
        scratch_shapes=[pltpu.VMEM((3, 2 * tq, tq + 2 * half), _f32)],
        compiler_params=_params(2),
    )(sink, q, k, v)


def _out0_kernel(x_ref, a_ref, o0_ref, l0_ref, o1_ref, l1_ref, o2_ref, l2_ref, wa_ref, wb_ref,
                 y_ref, o_scr, l_scr):
    tm = x_ref.shape[0]

    def unfold(src_ref, dst_scr, r):
        if r == 1:
            return src_ref[...].astype(_f32)
        for c in range(r):
            for t in range(2):
                lo = c * MXU_N + t * LANES
                dst_scr[t, pl.ds(c, tm // r, stride=r), :] = src_ref[:, lo:lo + LANES].astype(_f32)
        return jnp.concatenate([dst_scr[0], dst_scr[1]], axis=1)

    outs, lses = [], []
    for (o_ref, l_ref), (_, r) in zip(((o0_ref, l0_ref), (o1_ref, l1_ref), (o2_ref, l2_ref)), B_CONFIGS):
        outs.append(unfold(o_ref, o_scr, r))
        lses.append(unfold(l_ref, l_scr, r))
    m = jnp.maximum(jnp.maximum(lses[0], lses[1]), lses[2])
    es = [jnp.exp2(l - m) for l in lses]
    tot = es[0] + es[1] + es[2]
    b_out = (es[0] / tot) * outs[0] + (es[1] / tot) * outs[1] + (es[2] / tot) * outs[2]
    y = x_ref[...] + jnp.dot(a_ref[...], wa_ref[...], preferred_element_type=_f32)
    y_ref[...] = y + jnp.dot(b_out.astype(_bf16), wb_ref[...], preferred_element_type=_f32)


def _out0(x, a_out, b_outs, b_lses, wa, wb):
    t_rows = x.shape[0]
    tm = TM
    row = lambda i: (i, 0)
    in_specs = [pl.BlockSpec((tm, D_MODEL), row), pl.BlockSpec((tm, A_Q_DIM), row)]
    args = [x, a_out]
    for (_, r), o, l in zip(B_CONFIGS, b_outs, b_lses):
        in_specs += [pl.BlockSpec((tm // r, r * B_DIM), row)] * 2
        args += [o, l]
    in_specs += [_const_spec((A_Q_DIM, D_MODEL)), _const_spec((B_DIM, D_MODEL))]
    args += [wa, wb]
    return pl.pallas_call(
        _out0_kernel,
        name="out0",
        grid=(t_rows // tm,),
        in_specs=in_specs,
        out_specs=pl.BlockSpec((tm, D_MODEL), row),
        out_shape=jax.ShapeDtypeStruct((t_rows, D_MODEL), _f32),
        scratch_shapes=[pltpu.VMEM((2, tm, LANES), _f32), pltpu.VMEM((2, tm, LANES), _f32)],
        compiler_params=_params(1),
    )(*args)


def _ffn_kernel(xp_ref, x_ref, xn_ref, g_ref, wg_ref, wv_ref, cw_ref, wd_ref, y_ref,
                h_scr, gate_a, gate_b, val_a, val_b, acc_scr, *, tiles_per_seq):
    tm = x_ref.shape[0]
    i = pl.program_id(0)
    pos = i % tiles_per_seq
    keep_prev = jnp.where(pos == 0, 0.0, 1.0)
    keep_next = jnp.where(pos == tiles_per_seq - 1, 0.0, 1.0)
    g = g_ref[...]
    h_scr[0:HALO, :] = _rms(xp_ref[...], g).astype(_bf16)
    h_scr[HALO:HALO + tm, :] = _rms(x_ref[...], g).astype(_bf16)
    h_scr[HALO + tm:, :] = _rms(xn_ref[...], g).astype(_bf16)

    def up(j, gate_scr, val_scr):
        gate_scr[...] = jnp.dot(h_scr[...], wg_ref[j], preferred_element_type=_f32)
        gate_scr[HALO - 8:HALO, :] = gate_scr[HALO - 8:HALO, :] * keep_prev
        gate_scr[HALO + tm:HALO + tm + 8, :] = gate_scr[HALO + tm:HALO + tm + 8, :] * keep_next
        val_scr[...] = jnp.dot(h_scr[HALO:HALO + tm, :], wv_ref[j], preferred_element_type=_f32)

    def act(j, gate_scr, val_scr):
        cw = cw_ref[j]
        conv = (cw[0:1] * gate_scr[HALO - 1:HALO - 1 + tm, :]
                + cw[1:2] * gate_scr[HALO:HALO + tm, :]
                + cw[2:3] * gate_scr[HALO + 1:HALO + 1 + tm, :]
                + cw[3:4])
        return (jax.nn.gelu(conv) * val_scr[...]).astype(_bf16)

    bufs = ((gate_a, val_a), (gate_b, val_b))

    def run_chunks(j0, count, up_following):
        total = None
        for u in range(count):
            if u < count - 1 or up_following:
                up(j0 + u + 1, *bufs[(u + 1) % 2])
            a = act(j0 + u, *bufs[u % 2])
            d = jnp.dot(a, wd_ref[j0 + u], preferred_element_type=_f32)
            total = d if total is None else total + d
        return total

    def step(jj, carry):
        acc_scr[...] += run_chunks(FF_UNROLL * jj, FF_UNROLL, True)
        return carry

    n_loop = (N_FF_CHUNK - 1) // FF_UNROLL
    acc_scr[...] = jnp.zeros_like(acc_scr)
    up(0, gate_a, val_a)
    lax.fori_loop(0, n_loop, step, 0)
    tail = run_chunks(n_loop * FF_UNROLL, N_FF_CHUNK - n_loop * FF_UNROLL, False)
    y_ref[...] = x_ref[...] + (acc_scr[...] + tail)


def _ffn(x, seq, norm_g, wg, wv, cw, wd):
    t_rows = x.shape[0]
    tm = TM
    per = tm // HALO
    n_halo_blocks = t_rows // HALO
    return pl.pallas_call(
        functools.partial(_ffn_kernel, tiles_per_seq=seq // tm),
        name="conv_ffn",
        grid=(t_rows // tm,),
        in_specs=[pl.BlockSpec((HALO, D_MODEL), lambda i: (jnp.maximum(i * per - 1, 0), 0)),
                  pl.BlockSpec((tm, D_MODEL), lambda i: (i, 0)),
                  pl.BlockSpec((HALO, D_MODEL), lambda i: (jnp.minimum((i + 1) * per, n_halo_blocks - 1), 0)),
                  _const_spec((1, D_MODEL)),
                  _const_spec((N_FF_CHUNK, D_MODEL, FF_CHUNK)),
                  _const_spec((N_FF_CHUNK, D_MODEL, FF_CHUNK)),
                  _const_spec((N_FF_CHUNK, 8, FF_CHUNK)),
                  _const_spec((N_FF_CHUNK, FF_CHUNK, D_MODEL))],
        out_specs=pl.BlockSpec((tm, D_MODEL), lambda i: (i, 0)),
        out_shape=jax.ShapeDtypeStruct((t_rows, D_MODEL), _f32),
        scratch_shapes=[pltpu.VMEM((tm + 2 * HALO, D_MODEL), _bf16),
                        pltpu.VMEM((tm + 2 * HALO, FF_CHUNK), _f32),
                        pltpu.VMEM((tm + 2 * HALO, FF_CHUNK), _f32),
                        pltpu.VMEM((tm, FF_CHUNK), _f32),
                        pltpu.VMEM((tm, FF_CHUNK), _f32),
                        pltpu.VMEM((tm, D_MODEL), _f32)],
        compiler_params=_params(1),
    )(x, x, x, norm_g, wg, wv, cw, wd)


def _proj1_kernel(x_ref, g_ref, win_ref, qlg_ref, kvg_ref, wq_ref, wk_ref, wvt_ref, qg_ref, kg_ref,
                  gm_ref, cos_ref, sin_ref, q_ref, k_ref, vt_ref):
    tm = x_ref.shape[0]
    h = _rms(x_ref[...], g_ref[...]).astype(_bf16)
    proj = jnp.dot(h, win_ref[...], preferred_element_type=_f32)
    cq = _rms(proj[:, :C_Q_LORA], qlg_ref[...]).astype(_bf16)
    ckv_f32 = _rms(proj[:, C_Q_LORA:C_Q_LORA + C_KV_LORA], kvg_ref[...])
    ckv = ckv_f32.astype(_bf16)
    ckv_t = ckv_f32.T.astype(_bf16)
    k_rope = proj[:, C_Q_LORA + C_KV_LORA:]
    k_rope2 = jnp.concatenate([k_rope, k_rope], axis=1)
    cos = cos_ref[...]
    sin = sin_ref[...]
    gm = gm_ref[...]
    qg = jnp.concatenate([qg_ref[...]] * 2, axis=1)
    kg = jnp.concatenate([kg_ref[...]] * 2, axis=1)
    for b in range(C_HEADS // 2):
        cols = slice(b * MXU_N, (b + 1) * MXU_N)
        aq = jnp.dot(cq, wq_ref[:, cols], preferred_element_type=_f32)
        lo, hi = _norm_rope(aq, gm, qg, cos, sin, 1.0 / C_QK)
        q_ref[:, cols] = jnp.concatenate([lo, hi], axis=1).astype(_bf16)
        ak = jnp.dot(ckv, wk_ref[:, cols], preferred_element_type=_f32) + k_rope2
        lo, hi = _norm_rope(ak, gm, kg, cos, sin, 1.0 / C_QK)
        k_ref[:, cols] = jnp.concatenate([lo, hi], axis=1).astype(_bf16)
    r = lax.broadcasted_iota(jnp.int32, (MXU_N, tm), 0)
    for b in range(C_HEADS * C_VROWS // MXU_N):
        rows = slice(b * MXU_N, (b + 1) * MXU_N)
        vt = jnp.dot(wvt_ref[rows, :], ckv_t, preferred_element_type=_f32)
        ones_rows = (r + b * MXU_N) % C_VROWS >= C_V
        vt_ref[0, rows, :] = jnp.where(ones_rows, 1.0, vt).astype(_bf16)


def _proj1(x, seq, norm_g, win, qlg, kvg, wq, wk, wvt, qg, kg, gm, cos, sin):
    t_rows = x.shape[0]
    tm = TM
    n_pos_blocks = seq // tm
    row = lambda i: (i, 0)
    qk_w = C_HEADS * LANES
    v_w = C_HEADS * C_VROWS
    return pl.pallas_call(
        _proj1_kernel,
        name="proj1",
        grid=(t_rows // tm,),
        in_specs=[pl.BlockSpec((tm, D_MODEL), row),
                  _const_spec((1, D_MODEL)),
                  _const_spec(win.shape),
                  _const_spec((1, C_Q_LORA)), _const_spec((1, C_KV_LORA)),
                  _const_spec((C_Q_LORA, qk_w)), _const_spec((C_KV_LORA, qk_w)),
                  _const_spec((v_w, C_KV_LORA)),
                  _const_spec((1, LANES)), _const_spec((1, LANES)),
                  _const_spec((MXU_N, MXU_N)),
                  pl.BlockSpec((tm, LANES), lambda i: (i % n_pos_blocks, 0)),
                  pl.BlockSpec((tm, LANES), lambda i: (i % n_pos_blocks, 0))],
        out_specs=[pl.BlockSpec((tm, qk_w), row), pl.BlockSpec((tm, qk_w), row),
                   pl.BlockSpec((1, v_w, tm), lambda i: (i, 0, 0))],
        out_shape=[jax.ShapeDtypeStruct((t_rows, qk_w), _bf16),
                   jax.ShapeDtypeStruct((t_rows, qk_w), _bf16),
                   jax.ShapeDtypeStruct((t_rows // tm, v_w, tm), _bf16)],
        compiler_params=_params(1),
    )(x, norm_g, win, qlg, kvg, wq, wk, wvt, qg, kg, gm, cos, sin)


def _mla_kernel(q_ref, k_ref, vt_ref, o_ref, sa_ref, sb_ref, *, seq, tk, unroll):
    tq = q_ref.shape[1]
    n_tiles = seq // tk
    s_bufs = (sa_ref, sb_ref)

    def scores(t, buf, hh):
        ks = pl.multiple_of(t * tk, tk)
        kt = k_ref[0, pl.ds(ks, tk), hh * LANES:(hh + 1) * LANES]
        qh = q_ref[0, :, hh * LANES:(hh + 1) * LANES]
        s = lax.dot_general(kt, qh, (((1,), (1,)), ((), ())), preferred_element_type=_f32)
        buf[hh] = s
        return jnp.max(s, axis=0, keepdims=True)

    def consume(t, buf, hh, tmax, state):
        m_old, acc = state
        m_new = jnp.maximum(m_old, tmax)
        alpha = jnp.exp2(m_old - m_new)
        p = jnp.exp2(buf[hh] - m_new).astype(_bf16)
        vt = vt_ref[t, hh * C_VROWS:(hh + 1) * C_VROWS, :]
        acc = alpha * acc + jnp.dot(vt, p, preferred_element_type=_f32)
        return m_new, acc

    def run_tiles(t0, tmax, state, score_following):
        tmax, state = list(tmax), list(state)
        for u in range(unroll):
            score_next = u < unroll - 1 or score_following
            cur, nxt = s_bufs[u % 2], s_bufs[(u + 1) % 2]
            for hh in range(2):
                if score_next:
                    tmax_next = scores(t0 + u + 1, nxt, hh)
                state[hh] = consume(t0 + u, cur, hh, tmax[hh], state[hh])
                if score_next:
                    tmax[hh] = tmax_next
        return tuple(tmax), tuple(state)

    state = tuple((jnp.full((1, tq), -jnp.inf, _f32), jnp.zeros((C_VROWS, tq), _f32)) for _ in range(2))
    tmax = tuple(scores(0, s_bufs[0], hh) for hh in range(2))
    tmax, state = lax.fori_loop(
        0, n_tiles // unroll - 1,
        lambda jj, c: run_tiles(unroll * jj, c[0], c[1], True), (tmax, state))
    _, ((_, acc0), (_, acc1)) = run_tiles(n_tiles - unroll, tmax, state, False)
    out_t = jnp.concatenate([acc0[:C_V] / acc0[C_V:C_V + 1], acc1[:C_V] / acc1[C_V:C_V + 1]], axis=0)
    o_ref[0] = out_t.T.astype(o_ref.dtype)


def _mla_attn(q, k, vt):
    bsz, seq, _ = q.shape
    n_pairs = C_HEADS // 2
    tq, tk = MLA_TQ, MLA_TK
    n_tiles = seq // tk
    unroll = MLA_UNROLL if n_tiles >= 3 * MLA_UNROLL else 2
    assert vt.shape[2] == tk and n_tiles % unroll == 0 and seq % tq == 0
    return pl.pallas_call(
        functools.partial(_mla_kernel, seq=seq, tk=tk, unroll=unroll),
        name="mla_attn",
        grid=(bsz, n_pairs, seq // tq),
        in_specs=[pl.BlockSpec((1, tq, 2 * LANES), lambda b, p, i: (b, i, p)),
                  pl.BlockSpec((1, seq, 2 * LANES), lambda b, p, i: (b, 0, p)),
                  pl.BlockSpec((seq // tk, 2 * C_VROWS, tk), lambda b, p, i: (b, p, 0))],
        out_specs=pl.BlockSpec((1, tq, LANES), lambda b, p, i: (b, i, p)),
        out_shape=jax.ShapeDtypeStruct((bsz, seq, C_HEADS * C_V), _bf16),
        scratch_shapes=[pltpu.VMEM((2, tk, tq), _f32), pltpu.VMEM((2, tk, tq), _f32)],
        compiler_params=_params(3),
    )(q, k, vt)


def _out1_kernel(x_ref, o_ref, w_ref, y_ref):
    y_ref[...] = x_ref[...] + jnp.dot(o_ref[...], w_ref[...], preferred_element_type=_f32)


def _out1(x, o, w):
    t_rows = x.shape[0]
    tm = TM
    row = lambda i: (i, 0)
    return pl.pallas_call(
        _out1_kernel,
        name="out1",
        grid=(t_rows // tm,),
        in_specs=[pl.BlockSpec((tm, D_MODEL), row), pl.BlockSpec((tm, o.shape[1]), row),
                  _const_spec(w.shape)],
        out_specs=pl.BlockSpec((tm, D_MODEL), row),
        out_shape=jax.ShapeDtypeStruct((t_rows, D_MODEL), _f32),
        compiler_params=_params(1),
    )(x, o, w)


def _pair_cols(a, b):
    r = np.arange(HEAD_DIM // 2)
    return np.concatenate([a + r, b + r, a + HEAD_DIM // 2 + r, b + HEAD_DIM // 2 + r])


_PAIR_DIMS = _pair_cols(0, 0)


def _layer0_columns():
    nr, val = [], []
    for p in range(A_Q_HEADS // 2):
        nr.append(_pair_cols(2 * p * HEAD_DIM, (2 * p + 1) * HEAD_DIM))
    for h in range(A_KV_HEADS):
        nr.append(_pair_cols(A_Q_DIM + h * HEAD_DIM, A_Q_DIM + h * HEAD_DIM))
    b0 = A_Q_DIM + 2 * A_KV_DIM
    for g in range(N_BRANCH):
        for t in range(2):
            base = b0 + g * 3 * B_DIM + t * B_DIM
            for p in range(B_HEADS // 2):
                nr.append(_pair_cols(base + 2 * p * HEAD_DIM, base + (2 * p + 1) * HEAD_DIM))
    d = np.arange(HEAD_DIM)
    for h in range(A_KV_HEADS):
        base = A_Q_DIM + A_KV_DIM + h * HEAD_DIM
        val.append(np.concatenate([base + d, base + d]))
    for g in range(N_BRANCH):
        val.append(b0 + g * 3 * B_DIM + 2 * B_DIM + np.arange(B_DIM))
    return np.concatenate(nr + val)


def _layer0_gains(a_q_gain, a_k_gain, b_q_gain, b_k_gain):
    scale = (HEAD_DIM ** -0.5) * LOG2_E
    aq = a_q_gain[_PAIR_DIMS] * scale
    ak = a_k_gain[_PAIR_DIMS]
    parts = [aq] * (A_Q_HEADS // 2) + [ak] * A_KV_HEADS
    for g in range(N_BRANCH):
        parts += [b_q_gain[g][_PAIR_DIMS] * scale] * (B_HEADS // 2)
        parts += [b_k_gain[g][_PAIR_DIMS]] * (B_HEADS // 2)
    return jnp.concatenate(parts)[None, :]


def _layer0_group_matrix():
    lane = np.arange(MXU_N)
    head = (lane // LANES) * 2 + (lane // 32) % 2
    return jnp.asarray(head[:, None] == head[None, :], _bf16)


def _rope_tables(seq, dim):
    inv = jnp.power(ROPE_THETA, -jnp.arange(0, dim, 2, dtype=_f32) / dim)
    ang = jnp.arange(seq, dtype=_f32)[:, None] * inv[None, :]
    return jnp.cos(ang), jnp.sin(ang)


def _layer0_rope(seq):
    cos, sin = _rope_tables(seq, HEAD_DIM)
    return (jnp.concatenate([cos, cos, cos, cos], axis=1),
            jnp.concatenate([-sin, -sin, sin, sin], axis=1))


_C_HALF = C_ROPE // 2
_C_LANE_DIM = np.full(LANES, -1)
_C_LANE_DIM[0:_C_HALF] = C_NOPE + np.arange(_C_HALF)
_C_LANE_DIM[_C_HALF:_C_HALF + 32] = np.arange(32)
_C_LANE_DIM[64:64 + _C_HALF] = C_NOPE + _C_HALF + np.arange(_C_HALF)
_C_LANE_DIM[64 + _C_HALF:64 + _C_HALF + 32] = 32 + np.arange(32)


def _place_heads(w, per_head, lane_dim):
    idx = (np.arange(C_HEADS)[:, None] * per_head + np.maximum(lane_dim, 0)[None, :]).reshape(-1)
    keep = np.tile(lane_dim >= 0, C_HEADS)
    return jnp.where(jnp.asarray(keep)[None, :], w[:, idx], 0.0)


def _layer1_rope(seq):
    cos, sin = _rope_tables(seq, C_ROPE)
    ones = jnp.ones((seq, 64 - _C_HALF), _f32)
    zeros = jnp.zeros((seq, 64 - _C_HALF), _f32)
    return (jnp.concatenate([cos, ones, cos, ones], axis=1),
            jnp.concatenate([-sin, zeros, sin, zeros], axis=1))


def _layer1_group_matrix():
    lane = np.arange(MXU_N)
    return jnp.asarray((lane[:, None] // LANES) == (lane[None, :] // LANES), _bf16)


def _trunk(x3, e_norm, e_w_in, e_a_q_gain, e_a_k_gain, e_a_sink, e_b_q_gain, e_b_k_gain, e_w_out,
           o_norm, o_w_in, o_q_lora_gain, o_w_uq, o_kv_gain, o_w_ukv, o_q_gain, o_k_gain, o_w_out,
           f_norm, f_w_up, f_conv_w, f_conv_b, f_w_down):
    bsz, seq, _ = x3.shape
    assert seq % TM == 0
    x = x3.reshape(bsz * seq, D_MODEL)
    for layer in range(DEPTH):
        i = layer // 2
        if layer % 2 == 0:
            w0 = e_w_in[i][:, _layer0_columns()].astype(_bf16)
            cos, sin = _layer0_rope(seq)
            outs = _proj0(x, seq, e_norm[i][None, :], w0,
                          _layer0_gains(e_a_q_gain[i], e_a_k_gain[i], e_b_q_gain[i], e_b_k_gain[i]),
                          _layer0_group_matrix(), cos, sin)
            qa, kd, vd = outs[:3]
            a_out = _local_attn(qa.reshape(bsz, seq, A_Q_DIM), kd.reshape(bsz, seq, MXU_N),
                                vd.reshape(bsz, seq, MXU_N), e_a_sink[i].reshape(A_Q_HEADS // 2, 2),
                                half=A_HALF_WINDOW, kv_of_pair=lambda p: p // 2, want_lse=False)[0]
            b_outs, b_lses = [], []
            for g, (window, r) in enumerate(B_CONFIGS):
                bq, bk, bv = (t.reshape(bsz, seq // r, r * B_DIM) for t in outs[3 + 3 * g:6 + 3 * g])
                o, lse = _local_attn(bq, bk, bv, None, half=(window // 2) // r,
                                     kv_of_pair=None, want_lse=True)
                b_outs.append(o.reshape(bsz * seq // r, r * B_DIM))
                b_lses.append(lse.reshape(bsz * seq // r, r * B_DIM))
            w_out = e_w_out[i].astype(_bf16)
            x = _out0(x, a_out.reshape(bsz * seq, A_Q_DIM), b_outs, b_lses,
                      w_out[:A_Q_DIM], w_out[A_Q_DIM:])
        else:
            w_in = o_w_in[i]
            k_rope_cols = jnp.zeros((D_MODEL, LANES), _f32)
            k_rope_cols = k_rope_cols.at[:, 0:_C_HALF].set(w_in[:, C_Q_LORA + C_KV_LORA:C_Q_LORA + C_KV_LORA + _C_HALF])
            k_rope_cols = k_rope_cols.at[:, 64:64 + _C_HALF].set(w_in[:, C_Q_LORA + C_KV_LORA + _C_HALF:])
            win = jnp.concatenate([w_in[:, :C_Q_LORA + C_KV_LORA], k_rope_cols], axis=1).astype(_bf16)
            wq = _place_heads(o_w_uq[i], C_QK, _C_LANE_DIM).astype(_bf16)
            nope_only = np.where(_C_LANE_DIM < C_NOPE, _C_LANE_DIM, -1)
            wk = _place_heads(o_w_ukv[i], C_NOPE + C_V, nope_only).astype(_bf16)
            r_in_head = np.arange(C_HEADS * C_VROWS) % C_VROWS
            head = np.arange(C_HEADS * C_VROWS) // C_VROWS
            v_col = head * (C_NOPE + C_V) + C_NOPE + np.minimum(r_in_head, C_V - 1)
            wvt = jnp.where(jnp.asarray(r_in_head < C_V)[:, None], o_w_ukv[i][:, v_col].T, 0.0).astype(_bf16)
            lane_ok = jnp.asarray(_C_LANE_DIM >= 0)
            q_scale = (C_QK ** -0.5) * LOG2_E
            qg = jnp.where(lane_ok, o_q_gain[i][np.maximum(_C_LANE_DIM, 0)] * q_scale, 0.0)[None, :]
            kg = jnp.where(lane_ok, o_k_gain[i][np.maximum(_C_LANE_DIM, 0)], 0.0)[None, :]
            cos, sin = _layer1_rope(seq)
            q, k, vt = _proj1(x, seq, o_norm[i][None, :], win, o_q_lora_gain[i][None, :],
                              o_kv_gain[i][None, :], wq, wk, wvt, qg, kg, _layer1_group_matrix(),
                              cos, sin)
            o = _mla_attn(q.reshape(bsz, seq, -1), k.reshape(bsz, seq, -1), vt)
            x = _out1(x, o.reshape(bsz * seq, C_HEADS * C_V), o_w_out[i].astype(_bf16))
        w_up = f_w_up[layer]
        wg = w_up[:, :D_FF].reshape(D_MODEL, N_FF_CHUNK, FF_CHUNK).transpose(1, 0, 2).astype(_bf16)
        wv_ = w_up[:, D_FF:].reshape(D_MODEL, N_FF_CHUNK, FF_CHUNK).transpose(1, 0, 2).astype(_bf16)
        cw = jnp.concatenate([f_conv_w[layer], f_conv_b[layer][None, :], jnp.zeros((4, D_FF), _f32)], axis=0)
        cw = cw.reshape(8, N_FF_CHUNK, FF_CHUNK).transpose(1, 0, 2)
        wd = f_w_down[layer].reshape(N_FF_CHUNK, FF_CHUNK, D_MODEL).astype(_bf16)
        x = _ffn(x, seq, f_norm[layer][None, :], wg, wv_, cw, wd)
    return x.reshape(bsz, seq, D_MODEL)


def kernel(x_prompt, x_sample, e_norm, e_w_in, e_a_q_gain, e_a_k_gain, e_a_sink, e_b_q_gain, e_b_k_gain, e_w_out, o_norm, o_w_in, o_q_lora_gain, o_w_uq, o_kv_gain, o_w_ukv, o_q_gain, o_k_gain, o_w_out, f_norm, f_w_up, f_conv_w, f_conv_b, f_w_down):
    weights = (e_norm, e_w_in, e_a_q_gain, e_a_k_gain, e_a_sink, e_b_q_gain, e_b_k_gain, e_w_out,
               o_norm, o_w_in, o_q_lora_gain, o_w_uq, o_kv_gain, o_w_ukv, o_q_gain, o_k_gain, o_w_out,
               f_norm, f_w_up, f_conv_w, f_conv_b, f_w_down)
    return (_trunk(x_prompt, *weights), _trunk(x_sample, *weights))
```

```python
import functools

import numpy as np
import jax
import jax.numpy as jnp
from jax import lax
from jax.experimental import pallas as pl
from jax.experimental.pallas import tpu as pltpu

D_MODEL = 1024
HEAD_DIM = 64
ROPE_THETA = 10000.0
EPS = 1e-6
NEG = -1e30
A_Q_HEADS = 8
A_KV_HEADS = 2
A_HALF_WINDOW = 128
B_HEADS = 4
B_CONFIGS = ((128, 1), (512, 4), (2048, 16))
N_BRANCH = len(B_CONFIGS)
A_Q_DIM = A_Q_HEADS * HEAD_DIM
A_KV_DIM = A_KV_HEADS * HEAD_DIM
B_DIM = B_HEADS * HEAD_DIM
C_HEADS = 16
C_NOPE = 64
C_ROPE = 32
C_QK = C_NOPE + C_ROPE
C_V = 64
C_Q_LORA = 256
C_KV_LORA = 256
D_FF = 2816
CONV_W = 3
DEPTH = 2

LANES = 128
MXU_N = 256
F32_ROWS = 8
BF16_ROWS = 16
VMEM_LIMIT = 56 * 1024 * 1024
C_VROWS = C_V + BF16_ROWS

TM = 512
ATT_TQ = 128
ATT_UNROLL = 4
ATT_STEP_BYTES = 1 << 20
ATT_MAX_GROUP = 4
MLA_TQ = 512
MLA_TK = 512
MLA_UNROLL = 4
FF_CHUNK = 256
N_FF_CHUNK = D_FF // FF_CHUNK
FF_UNROLL = 4
HALO = BF16_ROWS

NR_BLOCKS = 9
P0_COLS = 13 * MXU_N

LOG2_E = float(np.log2(np.e))

_f32 = jnp.float32
_bf16 = jnp.bfloat16


def _const_spec(shape):
    nd = len(shape)
    return pl.BlockSpec(shape, lambda *_: (0,) * nd, pipeline_mode=pl.Buffered(1))


def _params(n_axes):
    return pltpu.CompilerParams(dimension_semantics=("arbitrary",) * n_axes,
                                vmem_limit_bytes=VMEM_LIMIT)


def _rms(x, g):
    y = x * lax.rsqrt(jnp.mean(x * x, axis=-1, keepdims=True) + EPS)
    return y * g


def _group_sumsq(a, gm):
    sq = a * a
    hi = sq.astype(_bf16)
    lo = (sq - hi.astype(_f32)).astype(_bf16)
    return (jnp.dot(hi, gm, preferred_element_type=_f32)
            + jnp.dot(lo, gm, preferred_element_type=_f32))


def _norm_rope(a, gm, gain, cos, sin, inv_dim):
    ss = _group_sumsq(a, gm)
    y = a * lax.rsqrt(ss * inv_dim + EPS) * gain
    halves = []
    for t in range(2):
        yt = y[:, t * LANES:(t + 1) * LANES]
        halves.append(yt * cos + pltpu.roll(yt, LANES // 2, 1) * sin)
    return halves


def _proj0_kernel(x_ref, g_ref, w_ref, hg_ref, gm_ref, cos_ref, sin_ref,
                  qa_ref, kd_ref, vd_ref,
                  bq0_ref, bk0_ref, bv0_ref, bq1_ref, bk1_ref, bv1_ref, bq2_ref, bk2_ref, bv2_ref,
                  h_scr, fold_scr):
    tm = x_ref.shape[0]
    h_scr[...] = _rms(x_ref[...], g_ref[...]).astype(_bf16)
    cos = cos_ref[...]
    sin = sin_ref[...]
    gm = gm_ref[...]

    blocks = {}

    def matmul_block(j):
        if j not in blocks:
            n_blk = min(2, P0_COLS // MXU_N - j)
            a = jnp.dot(h_scr[...], w_ref[:, j * MXU_N:(j + n_blk) * MXU_N], preferred_element_type=_f32)
            for t in range(n_blk):
                blocks[j + t] = a[:, t * MXU_N:(t + 1) * MXU_N]
        return blocks.pop(j)

    def store_folded(out_ref, val, r):
        if r == 1:
            out_ref[...] = val.astype(_bf16)
            return
        for t in range(2):
            fold_scr[t] = val[:, t * LANES:(t + 1) * LANES]
        for c in range(r):
            for t in range(2):
                lo = c * MXU_N + t * LANES
                out_ref[:, lo:lo + LANES] = fold_scr[t, pl.ds(c, tm // r, stride=r), :].astype(_bf16)

    def nr_block(j):
        lo, hi = _norm_rope(matmul_block(j), gm, hg_ref[:, j * MXU_N:(j + 1) * MXU_N],
                            cos, sin, 1.0 / HEAD_DIM)
        return jnp.concatenate([lo, hi], axis=1)

    qa_ref[:, 0:MXU_N] = nr_block(0).astype(_bf16)
    qa_ref[:, MXU_N:2 * MXU_N] = nr_block(1).astype(_bf16)
    kd_ref[...] = nr_block(2).astype(_bf16)
    b_refs = ((bq0_ref, bk0_ref, bv0_ref), (bq1_ref, bk1_ref, bv1_ref), (bq2_ref, bk2_ref, bv2_ref))
    for g, (_, r) in enumerate(B_CONFIGS):
        store_folded(b_refs[g][0], nr_block(3 + 2 * g), r)
        store_folded(b_refs[g][1], nr_block(4 + 2 * g), r)
    vd_ref[...] = matmul_block(NR_BLOCKS).astype(_bf16)
    for g, (_, r) in enumerate(B_CONFIGS):
        store_folded(b_refs[g][2], matmul_block(NR_BLOCKS + 1 + g), r)


def _proj0(x, seq, norm_g, w, head_gain, gm, cos, sin):
    t_rows = x.shape[0]
    tm = TM
    n_pos_blocks = seq // tm
    row = lambda i: (i, 0)
    out_shapes = [jax.ShapeDtypeStruct((t_rows, A_Q_DIM), _bf16),
                  jax.ShapeDtypeStruct((t_rows, MXU_N), _bf16),
                  jax.ShapeDtypeStruct((t_rows, MXU_N), _bf16)]
    out_specs = [pl.BlockSpec((tm, A_Q_DIM), row), pl.BlockSpec((tm, MXU_N), row),
                 pl.BlockSpec((tm, MXU_N), row)]
    for _, r in B_CONFIGS:
        for _ in range(3):
            out_shapes.append(jax.ShapeDtypeStruct((t_rows // r, r * B_DIM), _bf16))
            out_specs.append(pl.BlockSpec((tm // r, r * B_DIM), row))
    return pl.pallas_call(
        _proj0_kernel,
        name="proj0",
        grid=(t_rows // tm,),
        in_specs=[pl.BlockSpec((tm, D_MODEL), row),
                  _const_spec((1, D_MODEL)),
                  _const_spec((D_MODEL, P0_COLS)),
                  _const_spec((1, NR_BLOCKS * MXU_N)),
                  _const_spec((MXU_N, MXU_N)),
                  pl.BlockSpec((tm, LANES), lambda i: (i % n_pos_blocks, 0)),
                  pl.BlockSpec((tm, LANES), lambda i: (i % n_pos_blocks, 0))],
        out_specs=out_specs,
        out_shape=out_shapes,
        scratch_shapes=[pltpu.VMEM((tm, D_MODEL), _bf16), pltpu.VMEM((2, tm, LANES), _f32)],
        compiler_params=_params(1),
    )(x, norm_g, w, head_gain, gm, cos, sin)


def _local_attn_kernel(sink_ref, q_ref, k_ref, v_ref, *refs, seq, tq, half, group, has_sink, want_lse):
    o_ref = refs[0]
    bias_scr = refs[-1]
    win = tq + 2 * half
    lane = lax.broadcasted_iota(jnp.int32, (1, LANES), 1)
    first_qk = ((lane // (HEAD_DIM // 2)) % 2) == 0
    first_v = lane < HEAD_DIM
    row = lax.broadcasted_iota(jnp.int32, (2 * tq, 1), 0)
    top = row < tq
    qoff = jnp.where(top, row, row - tq)
    koff = lax.broadcasted_iota(jnp.int32, (1, win), 1)
    for e in range(3):
        bias_scr[e] = jnp.where(jnp.abs(e * half + qoff - koff) <= half, 0.0, NEG)

    def step(g, carry, lanes, sink):
        tiles = range(n_unroll)
        qs = [pl.multiple_of((g * n_unroll + u) * tq, tq) for u in tiles]
        ws = [pl.multiple_of(jnp.clip(qs[u] - half, 0, seq - win), half) for u in tiles]
        s = []
        for u in tiles:
            q2 = q_ref[0, pl.ds(qs[u], tq), lanes]
            zero = jnp.zeros_like(q2)
            qq = jnp.concatenate([jnp.where(first_qk, q2, zero), jnp.where(first_qk, zero, q2)], axis=0)
            kw = k_ref[0, pl.ds(ws[u], win), lanes]
            s.append(lax.dot_general(qq, kw, (((1,), (1,)), ((), ())), preferred_element_type=_f32))
        s = [s[u] + bias_scr[(qs[u] - ws[u]) // half] for u in tiles]
        m = [jnp.max(s[u], axis=-1, keepdims=True) for u in tiles]
        if has_sink:
            m = [jnp.maximum(m[u], sink) for u in tiles]
        p = [jnp.exp2(s[u] - m[u]) for u in tiles]
        den = [jnp.sum(p[u], axis=-1, keepdims=True) for u in tiles]
        if has_sink:
            den = [den[u] + jnp.exp2(sink - m[u]) for u in tiles]
        pv = [jnp.dot(p[u].astype(_bf16), v_ref[0, pl.ds(ws[u], win), lanes], preferred_element_type=_f32)
              for u in tiles]
        for u in tiles:
            od = pv[u] / den[u]
            o_ref[0, pl.ds(qs[u], tq), lanes] = jnp.where(first_v, od[:tq], od[tq:]).astype(o_ref.dtype)
            if want_lse:
                lse = m[u] + jnp.log2(den[u])
                refs[1][0, pl.ds(qs[u], tq), lanes] = jnp.where(first_v, lse[:tq], lse[tq:])
        return carry

    n_unroll = min(ATT_UNROLL, seq // tq)
    for pp in range(group):
        sink = None
        if has_sink:
            pair = pl.program_id(1) * group + pp
            sink = jnp.where(top, sink_ref[pair, 0], sink_ref[pair, 1]) * LOG2_E
        lax.fori_loop(0, seq // (tq * n_unroll),
                      functools.partial(step, lanes=slice(pp * LANES, (pp + 1) * LANES), sink=sink), 0)


def _local_attn(q, k, v, sink, *, half, kv_of_pair, want_lse):
    bsz, seq, width = q.shape
    n_pairs = width // LANES
    tq = ATT_TQ
    group = 1
    if kv_of_pair is None:
        kv_of_pair = lambda p: p
        while group < ATT_MAX_GROUP and n_pairs % (2 * group) == 0 and 2 * group * seq * LANES * 2 <= ATT_STEP_BYTES:
            group *= 2
    blk = group * LANES
    assert seq % (tq * min(ATT_UNROLL, seq // tq)) == 0 and seq >= tq + 2 * half
    has_sink = sink is not None
    if not has_sink:
        sink = jnp.zeros((1, 2), _f32)
    qmap = lambda b, p: (b, 0, p)
    kmap = lambda b, p: (b, 0, kv_of_pair(p))
    out_shape = [jax.ShapeDtypeStruct((bsz, seq, width), _bf16)]
    out_specs = [pl.BlockSpec((1, seq, blk), qmap)]
    if want_lse:
        out_shape.append(jax.ShapeDtypeStruct((bsz, seq, width), _f32))
        out_specs.append(pl.BlockSpec((1, seq, blk), qmap))
    return pl.pallas_call(
        functools.partial(_local_attn_kernel, seq=seq, tq=tq, half=half, group=group, has_sink=has_sink,
                          want_lse=want_lse),
        name="local_attn_h%d" % half,
        grid=(bsz, n_pairs // group),
        in_specs=[pl.BlockSpec(memory_space=pltpu.SMEM),
                  pl.BlockSpec((1, seq, blk), qmap),
                  pl.BlockSpec((1, seq, blk), kmap),
                  pl.BlockSpec((1, seq, blk), kmap)],
        out_specs=out_specs,
        out_shape=out_shape,
        scratch_shapes=[pltpu.VMEM((3, 2 * tq, tq + 2 * half), _f32)],
        compiler_params=_params(2),
    )(sink, q, k, v)


def _out0_kernel(x_ref, a_ref, o0_ref, l0_ref, o1_ref, l1_ref, o2_ref, l2_ref, wa_ref, wb_ref,
                 y_ref, o_scr, l_scr):
    tm = x_ref.shape[0]

    def unfold(src_ref, dst_scr, r):
        if r == 1:
            return src_ref[...].astype(_f32)
        for c in range(r):
            for t in range(2):
                lo = c * MXU_N + t * LANES
                dst_scr[t, pl.ds(c, tm // r, stride=r), :] = src_ref[:, lo:lo + LANES].astype(_f32)
        return jnp.concatenate([dst_scr[0], dst_scr[1]], axis=1)

    outs, lses = [], []
    for (o_ref, l_ref), (_, r) in zip(((o0_ref, l0_ref), (o1_ref, l1_ref), (o2_ref, l2_ref)), B_CONFIGS):
        outs.append(unfold(o_ref, o_scr, r))
        lses.append(unfold(l_ref, l_scr, r))
    m = jnp.maximum(jnp.maximum(lses[0], lses[1]), lses[2])
    es = [jnp.exp2(l - m) for l in lses]
    tot = es[0] + es[1] + es[2]
    b_out = (es[0] / tot) * outs[0] + (es[1] / tot) * outs[1] + (es[2] / tot) * outs[2]
    y = x_ref[...] + jnp.dot(a_ref[...], wa_ref[...], preferred_element_type=_f32)
    y_ref[...] = y + jnp.dot(b_out.astype(_bf16), wb_ref[...], preferred_element_type=_f32)


def _out0(x, a_out, b_outs, b_lses, wa, wb):
    t_rows = x.shape[0]
    tm = TM
    row = lambda i: (i, 0)
    in_specs = [pl.BlockSpec((tm, D_MODEL), row), pl.BlockSpec((tm, A_Q_DIM), row)]
    args = [x, a_out]
    for (_, r), o, l in zip(B_CONFIGS, b_outs, b_lses):
        in_specs += [pl.BlockSpec((tm // r, r * B_DIM), row)] * 2
        args += [o, l]
    in_specs += [_const_spec((A_Q_DIM, D_MODEL)), _const_spec((B_DIM, D_MODEL))]
    args += [wa, wb]
    return pl.pallas_call(
        _out0_kernel,
        name="out0",
        grid=(t_rows // tm,),
        in_specs=in_specs,
        out_specs=pl.BlockSpec((tm, D_MODEL), row),
        out_shape=jax.ShapeDtypeStruct((t_rows, D_MODEL), _f32),
        scratch_shapes=[pltpu.VMEM((2, tm, LANES), _f32), pltpu.VMEM((2, tm, LANES), _f32)],
        compiler_params=_params(1),
    )(*args)


def _ffn_kernel(xp_ref, x_ref, xn_ref, g_ref, wg_ref, wv_ref, cw_ref, wd_ref, y_ref,
                h_scr, gate_a, gate_b, val_a, val_b, acc_scr, *, tiles_per_seq):
    tm = x_ref.shape[0]
    i = pl.program_id(0)
    pos = i % tiles_per_seq
    keep_prev = jnp.where(pos == 0, 0.0, 1.0)
    keep_next = jnp.where(pos == tiles_per_seq - 1, 0.0, 1.0)
    g = g_ref[...]
    h_scr[0:HALO, :] = _rms(xp_ref[...], g).astype(_bf16)
    h_scr[HALO:HALO + tm, :] = _rms(x_ref[...], g).astype(_bf16)
    h_scr[HALO + tm:, :] = _rms(xn_ref[...], g).astype(_bf16)

    def up(j, gate_scr, val_scr):
        gate_scr[...] = jnp.dot(h_scr[...], wg_ref[j], preferred_element_type=_f32)
        gate_scr[HALO - F32_ROWS:HALO, :] = gate_scr[HALO - F32_ROWS:HALO, :] * keep_prev
        gate_scr[HALO + tm:HALO + tm + F32_ROWS, :] = gate_scr[HALO + tm:HALO + tm + F32_ROWS, :] * keep_next
        val_scr[...] = jnp.dot(h_scr[HALO:HALO + tm, :], wv_ref[j], preferred_element_type=_f32)

    def act(j, gate_scr, val_scr):
        cw = cw_ref[j]
        conv = (cw[0:1] * gate_scr[HALO - 1:HALO - 1 + tm, :]
                + cw[1:2] * gate_scr[HALO:HALO + tm, :]
                + cw[2:3] * gate_scr[HALO + 1:HALO + 1 + tm, :]
                + cw[3:4])
        return (jax.nn.gelu(conv) * val_scr[...]).astype(_bf16)

    bufs = ((gate_a, val_a), (gate_b, val_b))

    def run_chunks(j0, count, up_following):
        total = None
        for u in range(count):
            if u < count - 1 or up_following:
                up(j0 + u + 1, *bufs[(u + 1) % 2])
            a = act(j0 + u, *bufs[u % 2])
            d = jnp.dot(a, wd_ref[j0 + u], preferred_element_type=_f32)
            total = d if total is None else total + d
        return total

    def step(jj, carry):
        acc_scr[...] += run_chunks(FF_UNROLL * jj, FF_UNROLL, True)
        return carry

    n_loop = (N_FF_CHUNK - 1) // FF_UNROLL
    acc_scr[...] = jnp.zeros_like(acc_scr)
    up(0, gate_a, val_a)
    lax.fori_loop(0, n_loop, step, 0)
    tail = run_chunks(n_loop * FF_UNROLL, N_FF_CHUNK - n_loop * FF_UNROLL, False)
    y_ref[...] = x_ref[...] + (acc_scr[...] + tail)


def _ffn(x, seq, norm_g, wg, wv, cw, wd):
    t_rows = x.shape[0]
    tm = TM
    per = tm // HALO
    n_halo_blocks = t_rows // HALO
    return pl.pallas_call(
        functools.partial(_ffn_kernel, tiles_per_seq=seq // tm),
        name="conv_ffn",
        grid=(t_rows // tm,),
        in_specs=[pl.BlockSpec((HALO, D_MODEL), lambda i: (jnp.maximum(i * per - 1, 0), 0)),
                  pl.BlockSpec((tm, D_MODEL), lambda i: (i, 0)),
                  pl.BlockSpec((HALO, D_MODEL), lambda i: (jnp.minimum((i + 1) * per, n_halo_blocks - 1), 0)),
                  _const_spec((1, D_MODEL)),
                  _const_spec((N_FF_CHUNK, D_MODEL, FF_CHUNK)),
                  _const_spec((N_FF_CHUNK, D_MODEL, FF_CHUNK)),
                  _const_spec((N_FF_CHUNK, F32_ROWS, FF_CHUNK)),
                  _const_spec((N_FF_CHUNK, FF_CHUNK, D_MODEL))],
        out_specs=pl.BlockSpec((tm, D_MODEL), lambda i: (i, 0)),
        out_shape=jax.ShapeDtypeStruct((t_rows, D_MODEL), _f32),
        scratch_shapes=[pltpu.VMEM((tm + 2 * HALO, D_MODEL), _bf16),
                        pltpu.VMEM((tm + 2 * HALO, FF_CHUNK), _f32),
                        pltpu.VMEM((tm + 2 * HALO, FF_CHUNK), _f32),
                        pltpu.VMEM((tm, FF_CHUNK), _f32),
                        pltpu.VMEM((tm, FF_CHUNK), _f32),
                        pltpu.VMEM((tm, D_MODEL), _f32)],
        compiler_params=_params(1),
    )(x, x, x, norm_g, wg, wv, cw, wd)


def _proj1_kernel(x_ref, g_ref, win_ref, qlg_ref, kvg_ref, wq_ref, wk_ref, wvt_ref, qg_ref, kg_ref,
                  gm_ref, cos_ref, sin_ref, q_ref, k_ref, vt_ref):
    tm = x_ref.shape[0]
    h = _rms(x_ref[...], g_ref[...]).astype(_bf16)
    proj = jnp.dot(h, win_ref[...], preferred_element_type=_f32)
    cq = _rms(proj[:, :C_Q_LORA], qlg_ref[...]).astype(_bf16)
    ckv_f32 = _rms(proj[:, C_Q_LORA:C_Q_LORA + C_KV_LORA], kvg_ref[...])
    ckv = ckv_f32.astype(_bf16)
    ckv_t = ckv_f32.T.astype(_bf16)
    k_rope = proj[:, C_Q_LORA + C_KV_LORA:]
    k_rope2 = jnp.concatenate([k_rope, k_rope], axis=1)
    cos = cos_ref[...]
    sin = sin_ref[...]
    gm = gm_ref[...]
    qg = jnp.concatenate([qg_ref[...]] * 2, axis=1)
    kg = jnp.concatenate([kg_ref[...]] * 2, axis=1)
    for b in range(C_HEADS // 2):
        cols = slice(b * MXU_N, (b + 1) * MXU_N)
        aq = jnp.dot(cq, wq_ref[:, cols], preferred_element_type=_f32)
        lo, hi = _norm_rope(aq, gm, qg, cos, sin, 1.0 / C_QK)
        q_ref[:, cols] = jnp.concatenate([lo, hi], axis=1).astype(_bf16)
        ak = jnp.dot(ckv, wk_ref[:, cols], preferred_element_type=_f32) + k_rope2
        lo, hi = _norm_rope(ak, gm, kg, cos, sin, 1.0 / C_QK)
        k_ref[:, cols] = jnp.concatenate([lo, hi], axis=1).astype(_bf16)
    r = lax.broadcasted_iota(jnp.int32, (MXU_N, tm), 0)
    for b in range(C_HEADS * C_VROWS // MXU_N):
        rows = slice(b * MXU_N, (b + 1) * MXU_N)
        vt = jnp.dot(wvt_ref[rows, :], ckv_t, preferred_element_type=_f32)
        ones_rows = (r + b * MXU_N) % C_VROWS >= C_V
        vt_ref[0, rows, :] = jnp.where(ones_rows, 1.0, vt).astype(_bf16)


def _proj1(x, seq, norm_g, win, qlg, kvg, wq, wk, wvt, qg, kg, gm, cos, sin):
    t_rows = x.shape[0]
    tm = TM
    n_pos_blocks = seq // tm
    row = lambda i: (i, 0)
    qk_w = C_HEADS * LANES
    v_w = C_HEADS * C_VROWS
    return pl.pallas_call(
        _proj1_kernel,
        name="proj1",
        grid=(t_rows // tm,),
        in_specs=[pl.BlockSpec((tm, D_MODEL), row),
                  _const_spec((1, D_MODEL)),
                  _const_spec(win.shape),
                  _const_spec((1, C_Q_LORA)), _const_spec((1, C_KV_LORA)),
                  _const_spec((C_Q_LORA, qk_w)), _const_spec((C_KV_LORA, qk_w)),
                  _const_spec((v_w, C_KV_LORA)),
                  _const_spec((1, LANES)), _const_spec((1, LANES)),
                  _const_spec((MXU_N, MXU_N)),
                  pl.BlockSpec((tm, LANES), lambda i: (i % n_pos_blocks, 0)),
                  pl.BlockSpec((tm, LANES), lambda i: (i % n_pos_blocks, 0))],
        out_specs=[pl.BlockSpec((tm, qk_w), row), pl.BlockSpec((tm, qk_w), row),
                   pl.BlockSpec((1, v_w, tm), lambda i: (i, 0, 0))],
        out_shape=[jax.ShapeDtypeStruct((t_rows, qk_w), _bf16),
                   jax.ShapeDtypeStruct((t_rows, qk_w), _bf16),
                   jax.ShapeDtypeStruct((t_rows // tm, v_w, tm), _bf16)],
        compiler_params=_params(1),
    )(x, norm_g, win, qlg, kvg, wq, wk, wvt, qg, kg, gm, cos, sin)


def _mla_kernel(q_ref, k_ref, vt_ref, o_ref, sa_ref, sb_ref, *, seq, tk, unroll):
    tq = q_ref.shape[1]
    n_tiles = seq // tk
    s_bufs = (sa_ref, sb_ref)

    def scores(t, buf, hh):
        ks = pl.multiple_of(t * tk, tk)
        kt = k_ref[0, pl.ds(ks, tk), hh * LANES:(hh + 1) * LANES]
        qh = q_ref[0, :, hh * LANES:(hh + 1) * LANES]
        s = lax.dot_general(kt, qh, (((1,), (1,)), ((), ())), preferred_element_type=_f32)
        buf[hh] = s
        return jnp.max(s, axis=0, keepdims=True)

    def consume(t, buf, hh, tmax, state):
        m_old, acc = state
        m_new = jnp.maximum(m_old, tmax)
        alpha = jnp.exp2(m_old - m_new)
        p = jnp.exp2(buf[hh] - m_new).astype(_bf16)
        vt = vt_ref[t, hh * C_VROWS:(hh + 1) * C_VROWS, :]
        acc = alpha * acc + jnp.dot(vt, p, preferred_element_type=_f32)
        return m_new, acc

    def run_tiles(t0, tmax, state, score_following):
        tmax, state = list(tmax), list(state)
        for u in range(unroll):
            score_next = u < unroll - 1 or score_following
            cur, nxt = s_bufs[u % 2], s_bufs[(u + 1) % 2]
            for hh in range(2):
                if score_next:
                    tmax_next = scores(t0 + u + 1, nxt, hh)
                state[hh] = consume(t0 + u, cur, hh, tmax[hh], state[hh])
                if score_next:
                    tmax[hh] = tmax_next
        return tuple(tmax), tuple(state)

    state = tuple((jnp.full((1, tq), -jnp.inf, _f32), jnp.zeros((C_VROWS, tq), _f32)) for _ in range(2))
    tmax = tuple(scores(0, s_bufs[0], hh) for hh in range(2))
    tmax, state = lax.fori_loop(
        0, n_tiles // unroll - 1,
        lambda jj, c: run_tiles(unroll * jj, c[0], c[1], True), (tmax, state))
    _, ((_, acc0), (_, acc1)) = run_tiles(n_tiles - unroll, tmax, state, False)
    out_t = jnp.concatenate([acc0[:C_V] / acc0[C_V:C_V + 1], acc1[:C_V] / acc1[C_V:C_V + 1]], axis=0)
    o_ref[0] = out_t.T.astype(o_ref.dtype)


def _mla_attn(q, k, vt):
    bsz, seq, _ = q.shape
    n_pairs = C_HEADS // 2
    tq, tk = MLA_TQ, MLA_TK
    n_tiles = seq // tk
    unroll = MLA_UNROLL if n_tiles >= 3 * MLA_UNROLL else 2
    assert vt.shape[2] == tk and n_tiles % unroll == 0 and seq % tq == 0
    return pl.pallas_call(
        functools.partial(_mla_kernel, seq=seq, tk=tk, unroll=unroll),
        name="mla_attn",
        grid=(bsz, n_pairs, seq // tq),
        in_specs=[pl.BlockSpec((1, tq, 2 * LANES), lambda b, p, i: (b, i, p)),
                  pl.BlockSpec((1, seq, 2 * LANES), lambda b, p, i: (b, 0, p)),
                  pl.BlockSpec((seq // tk, 2 * C_VROWS, tk), lambda b, p, i: (b, p, 0))],
        out_specs=pl.BlockSpec((1, tq, LANES), lambda b, p, i: (b, i, p)),
        out_shape=jax.ShapeDtypeStruct((bsz, seq, C_HEADS * C_V), _bf16),
        scratch_shapes=[pltpu.VMEM((2, tk, tq), _f32), pltpu.VMEM((2, tk, tq), _f32)],
        compiler_params=_params(3),
    )(q, k, vt)


def _out1_kernel(x_ref, o_ref, w_ref, y_ref):
    y_ref[...] = x_ref[...] + jnp.dot(o_ref[...], w_ref[...], preferred_element_type=_f32)


def _out1(x, o, w):
    t_rows = x.shape[0]
    tm = TM
    row = lambda i: (i, 0)
    return pl.pallas_call(
        _out1_kernel,
        name="out1",
        grid=(t_rows // tm,),
        in_specs=[pl.BlockSpec((tm, D_MODEL), row), pl.BlockSpec((tm, o.shape[1]), row),
                  _const_spec(w.shape)],
        out_specs=pl.BlockSpec((tm, D_MODEL), row),
        out_shape=jax.ShapeDtypeStruct((t_rows, D_MODEL), _f32),
        compiler_params=_params(1),
    )(x, o, w)


def _pair_cols(a, b):
    r = np.arange(HEAD_DIM // 2)
    return np.concatenate([a + r, b + r, a + HEAD_DIM // 2 + r, b + HEAD_DIM // 2 + r])


_PAIR_DIMS = _pair_cols(0, 0)


def _layer0_columns():
    nr, val = [], []
    for p in range(A_Q_HEADS // 2):
        nr.append(_pair_cols(2 * p * HEAD_DIM, (2 * p + 1) * HEAD_DIM))
    for h in range(A_KV_HEADS):
        nr.append(_pair_cols(A_Q_DIM + h * HEAD_DIM, A_Q_DIM + h * HEAD_DIM))
    b0 = A_Q_DIM + 2 * A_KV_DIM
    for g in range(N_BRANCH):
        for t in range(2):
            base = b0 + g * 3 * B_DIM + t * B_DIM
            for p in range(B_HEADS // 2):
                nr.append(_pair_cols(base + 2 * p * HEAD_DIM, base + (2 * p + 1) * HEAD_DIM))
    d = np.arange(HEAD_DIM)
    for h in range(A_KV_HEADS):
        base = A_Q_DIM + A_KV_DIM + h * HEAD_DIM
        val.append(np.concatenate([base + d, base + d]))
    for g in range(N_BRANCH):
        val.append(b0 + g * 3 * B_DIM + 2 * B_DIM + np.arange(B_DIM))
    return np.concatenate(nr + val)


def _layer0_gains(a_q_gain, a_k_gain, b_q_gain, b_k_gain):
    scale = (HEAD_DIM ** -0.5) * LOG2_E
    aq = a_q_gain[_PAIR_DIMS] * scale
    ak = a_k_gain[_PAIR_DIMS]
    parts = [aq] * (A_Q_HEADS // 2) + [ak] * A_KV_HEADS
    for g in range(N_BRANCH):
        parts += [b_q_gain[g][_PAIR_DIMS] * scale] * (B_HEADS // 2)
        parts += [b_k_gain[g][_PAIR_DIMS]] * (B_HEADS // 2)
    return jnp.concatenate(parts)[None, :]


def _layer0_group_matrix():
    lane = np.arange(MXU_N)
    head = (lane // LANES) * 2 + (lane // (HEAD_DIM // 2)) % 2
    return jnp.asarray(head[:, None] == head[None, :], _bf16)


def _rope_tables(seq, dim):
    inv = jnp.power(ROPE_THETA, -jnp.arange(0, dim, 2, dtype=_f32) / dim)
    ang = jnp.arange(seq, dtype=_f32)[:, None] * inv[None, :]
    return jnp.cos(ang), jnp.sin(ang)


def _layer0_rope(seq):
    cos, sin = _rope_tables(seq, HEAD_DIM)
    return (jnp.concatenate([cos, cos, cos, cos], axis=1),
            jnp.concatenate([-sin, -sin, sin, sin], axis=1))


_C_HALF = C_ROPE // 2
_C_LANE_DIM = np.full(LANES, -1)
_C_NOPE_HALF = C_NOPE // 2
_HALF_BLOCK = LANES // 2
_C_LANE_DIM[0:_C_HALF] = C_NOPE + np.arange(_C_HALF)
_C_LANE_DIM[_C_HALF:_C_HALF + _C_NOPE_HALF] = np.arange(_C_NOPE_HALF)
_C_LANE_DIM[_HALF_BLOCK:_HALF_BLOCK + _C_HALF] = C_NOPE + _C_HALF + np.arange(_C_HALF)
_C_LANE_DIM[_HALF_BLOCK + _C_HALF:_HALF_BLOCK + _C_HALF + _C_NOPE_HALF] = _C_NOPE_HALF + np.arange(_C_NOPE_HALF)


def _place_heads(w, per_head, lane_dim):
    idx = (np.arange(C_HEADS)[:, None] * per_head + np.maximum(lane_dim, 0)[None, :]).reshape(-1)
    keep = np.tile(lane_dim >= 0, C_HEADS)
    return jnp.where(jnp.asarray(keep)[None, :], w[:, idx], 0.0)


def _layer1_rope(seq):
    cos, sin = _rope_tables(seq, C_ROPE)
    ones = jnp.ones((seq, _HALF_BLOCK - _C_HALF), _f32)
    zeros = jnp.zeros((seq, _HALF_BLOCK - _C_HALF), _f32)
    return (jnp.concatenate([cos, ones, cos, ones], axis=1),
            jnp.concatenate([-sin, zeros, sin, zeros], axis=1))


def _layer1_group_matrix():
    lane = np.arange(MXU_N)
    return jnp.asarray((lane[:, None] // LANES) == (lane[None, :] // LANES), _bf16)


def _trunk(x3, e_norm, e_w_in, e_a_q_gain, e_a_k_gain, e_a_sink, e_b_q_gain, e_b_k_gain, e_w_out,
           o_norm, o_w_in, o_q_lora_gain, o_w_uq, o_kv_gain, o_w_ukv, o_q_gain, o_k_gain, o_w_out,
           f_norm, f_w_up, f_conv_w, f_conv_b, f_w_down):
    bsz, seq, _ = x3.shape
    assert seq % TM == 0
    x = x3.reshape(bsz * seq, D_MODEL)
    for layer in range(DEPTH):
        i = layer // 2
        if layer % 2 == 0:
            w0 = e_w_in[i][:, _layer0_columns()].astype(_bf16)
            cos, sin = _layer0_rope(seq)
            outs = _proj0(x, seq, e_norm[i][None, :], w0,
                          _layer0_gains(e_a_q_gain[i], e_a_k_gain[i], e_b_q_gain[i], e_b_k_gain[i]),
                          _layer0_group_matrix(), cos, sin)
            qa, kd, vd = outs[:3]
            a_out = _local_attn(qa.reshape(bsz, seq, A_Q_DIM), kd.reshape(bsz, seq, MXU_N),
                                vd.reshape(bsz, seq, MXU_N), e_a_sink[i].reshape(A_Q_HEADS // 2, 2),
                                half=A_HALF_WINDOW, kv_of_pair=lambda p: p // 2, want_lse=False)[0]
            b_outs, b_lses = [], []
            for g, (window, r) in enumerate(B_CONFIGS):
                bq, bk, bv = (t.reshape(bsz, seq // r, r * B_DIM) for t in outs[3 + 3 * g:6 + 3 * g])
                o, lse = _local_attn(bq, bk, bv, None, half=(window // 2) // r,
                                     kv_of_pair=None, want_lse=True)
                b_outs.append(o.reshape(bsz * seq // r, r * B_DIM))
                b_lses.append(lse.reshape(bsz * seq // r, r * B_DIM))
            w_out = e_w_out[i].astype(_bf16)
            x = _out0(x, a_out.reshape(bsz * seq, A_Q_DIM), b_outs, b_lses,
                      w_out[:A_Q_DIM], w_out[A_Q_DIM:])
        else:
            w_in = o_w_in[i]
            k_rope_cols = jnp.zeros((D_MODEL, LANES), _f32)
            k_rope_cols = k_rope_cols.at[:, 0:_C_HALF].set(w_in[:, C_Q_LORA + C_KV_LORA:C_Q_LORA + C_KV_LORA + _C_HALF])
            k_rope_cols = k_rope_cols.at[:, _HALF_BLOCK:_HALF_BLOCK + _C_HALF].set(
                w_in[:, C_Q_LORA + C_KV_LORA + _C_HALF:])
            win = jnp.concatenate([w_in[:, :C_Q_LORA + C_KV_LORA], k_rope_cols], axis=1).astype(_bf16)
            wq = _place_heads(o_w_uq[i], C_QK, _C_LANE_DIM).astype(_bf16)
            nope_only = np.where(_C_LANE_DIM < C_NOPE, _C_LANE_DIM, -1)
            wk = _place_heads(o_w_ukv[i], C_NOPE + C_V, nope_only).astype(_bf16)
            r_in_head = np.arange(C_HEADS * C_VROWS) % C_VROWS
            head = np.arange(C_HEADS * C_VROWS) // C_VROWS
            v_col = head * (C_NOPE + C_V) + C_NOPE + np.minimum(r_in_head, C_V - 1)
            wvt = jnp.where(jnp.asarray(r_in_head < C_V)[:, None], o_w_ukv[i][:, v_col].T, 0.0).astype(_bf16)
            lane_ok = jnp.asarray(_C_LANE_DIM >= 0)
            q_scale = (C_QK ** -0.5) * LOG2_E
            qg = jnp.where(lane_ok, o_q_gain[i][np.maximum(_C_LANE_DIM, 0)] * q_scale, 0.0)[None, :]
            kg = jnp.where(lane_ok, o_k_gain[i][np.maximum(_C_LANE_DIM, 0)], 0.0)[None, :]
            cos, sin = _layer1_rope(seq)
            q, k, vt = _proj1(x, seq, o_norm[i][None, :], win, o_q_lora_gain[i][None, :],
                              o_kv_gain[i][None, :], wq, wk, wvt, qg, kg, _layer1_group_matrix(),
                              cos, sin)
            o = _mla_attn(q.reshape(bsz, seq, -1), k.reshape(bsz, seq, -1), vt)
            x = _out1(x, o.reshape(bsz * seq, C_HEADS * C_V), o_w_out[i].astype(_bf16))
        w_up = f_w_up[layer]
        wg = w_up[:, :D_FF].reshape(D_MODEL, N_FF_CHUNK, FF_CHUNK).transpose(1, 0, 2).astype(_bf16)
        wv_ = w_up[:, D_FF:].reshape(D_MODEL, N_FF_CHUNK, FF_CHUNK).transpose(1, 0, 2).astype(_bf16)
        cw = jnp.concatenate([f_conv_w[layer], f_conv_b[layer][None, :],
                              jnp.zeros((F32_ROWS - CONV_W - 1, D_FF), _f32)], axis=0)
        cw = cw.reshape(F32_ROWS, N_FF_CHUNK, FF_CHUNK).transpose(1, 0, 2)
        wd = f_w_down[layer].reshape(N_FF_CHUNK, FF_CHUNK, D_MODEL).astype(_bf16)
        x = _ffn(x, seq, f_norm[layer][None, :], wg, wv_, cw, wd)
    return x.reshape(bsz, seq, D_MODEL)


def kernel(x_prompt, x_sample, e_norm, e_w_in, e_a_q_gain, e_a_k_gain, e_a_sink, e_b_q_gain, e_b_k_gain, e_w_out, o_norm, o_w_in, o_q_lora_gain, o_w_uq, o_kv_gain, o_w_ukv, o_q_gain, o_k_gain, o_w_out, f_norm, f_w_up, f_conv_w, f_conv_b, f_w_down):
    weights = (e_norm, e_w_in, e_a_q_gain, e_a_k_gain, e_a_sink, e_b_q_gain, e_b_k_gain, e_w_out,
               o_norm, o_w_in, o_q_lora_gain, o_w_uq, o_kv_gain, o_w_ukv, o_q_gain, o_k_gain, o_w_out,
               f_norm, f_w_up, f_conv_w, f_conv_b, f_w_down)
    return (_trunk(x_prompt, *weights), _trunk(x_sample, *weights))
```

```python
import functools

import numpy as np
import jax
import jax.numpy as jnp
from jax import lax
from jax.experimental import pallas as pl
from jax.experimental.pallas import tpu as pltpu

D_MODEL = 1024
HEAD_DIM = 64
ROPE_THETA = 10000.0
EPS = 1e-6
NEG = -1e30
A_Q_HEADS = 8
A_KV_HEADS = 2
A_HALF_WINDOW = 128
B_HEADS = 4
B_CONFIGS = ((128, 1), (512, 4), (2048, 16))
N_BRANCH = len(B_CONFIGS)
A_Q_DIM = A_Q_HEADS * HEAD_DIM
A_KV_DIM = A_KV_HEADS * HEAD_DIM
B_DIM = B_HEADS * HEAD_DIM
C_HEADS = 16
C_NOPE = 64
C_ROPE = 32
C_QK = C_NOPE + C_ROPE
C_V = 64
C_Q_LORA = 256
C_KV_LORA = 256
D_FF = 2816
CONV_W = 3
DEPTH = 2

LANES = 128
MXU_N = 256
F32_ROWS = 8
BF16_ROWS = 16
VMEM_LIMIT = 56 * 1024 * 1024
C_VROWS = C_V + BF16_ROWS

TM = 512
ATT_TQ = 128
ATT_UNROLL = 8
ATT_STEP_BYTES = 1 << 20
ATT_MAX_GROUP = 4
MLA_TQ = 512
MLA_TK = 512
MLA_UNROLL = 4
FF_CHUNK = 256
N_FF_CHUNK = D_FF // FF_CHUNK
FF_UNROLL = 4
HALO = BF16_ROWS

NR_BLOCKS = 9
P0_COLS = 13 * MXU_N

LOG2_E = float(np.log2(np.e))

_f32 = jnp.float32
_bf16 = jnp.bfloat16


def _const_spec(shape):
    nd = len(shape)
    return pl.BlockSpec(shape, lambda *_: (0,) * nd, pipeline_mode=pl.Buffered(1))


def _params(n_axes):
    return pltpu.CompilerParams(dimension_semantics=("arbitrary",) * n_axes,
                                vmem_limit_bytes=VMEM_LIMIT)


def _rms(x, g):
    y = x * lax.rsqrt(jnp.mean(x * x, axis=-1, keepdims=True) + EPS)
    return y * g


def _group_sumsq(a, gm):
    sq = a * a
    hi = sq.astype(_bf16)
    lo = (sq - hi.astype(_f32)).astype(_bf16)
    return (jnp.dot(hi, gm, preferred_element_type=_f32)
            + jnp.dot(lo, gm, preferred_element_type=_f32))


def _norm_rope(a, gm, gain, cos, sin, inv_dim):
    ss = _group_sumsq(a, gm)
    y = a * lax.rsqrt(ss * inv_dim + EPS) * gain
    halves = []
    for t in range(2):
        yt = y[:, t * LANES:(t + 1) * LANES]
        halves.append(yt * cos + pltpu.roll(yt, LANES // 2, 1) * sin)
    return halves


def _proj0_kernel(x_ref, g_ref, w_ref, hg_ref, gm_ref, cos_ref, sin_ref,
                  qa_ref, kd_ref, vd_ref,
                  bq0_ref, bk0_ref, bv0_ref, bq1_ref, bk1_ref, bv1_ref, bq2_ref, bk2_ref, bv2_ref,
                  h_scr, fold_scr):
    tm = x_ref.shape[0]
    h_scr[...] = _rms(x_ref[...], g_ref[...]).astype(_bf16)
    cos = cos_ref[...]
    sin = sin_ref[...]
    gm = gm_ref[...]

    blocks = {}

    def matmul_block(j):
        if j not in blocks:
            n_blk = min(2, P0_COLS // MXU_N - j)
            a = jnp.dot(h_scr[...], w_ref[:, j * MXU_N:(j + n_blk) * MXU_N], preferred_element_type=_f32)
            for t in range(n_blk):
                blocks[j + t] = a[:, t * MXU_N:(t + 1) * MXU_N]
        return blocks.pop(j)

    def store_folded(out_ref, val, r):
        if r == 1:
            out_ref[...] = val.astype(_bf16)
            return
        for t in range(2):
            fold_scr[t] = val[:, t * LANES:(t + 1) * LANES]
        for c in range(r):
            for t in range(2):
                lo = c * MXU_N + t * LANES
                out_ref[:, lo:lo + LANES] = fold_scr[t, pl.ds(c, tm // r, stride=r), :].astype(_bf16)

    def nr_block(j):
        lo, hi = _norm_rope(matmul_block(j), gm, hg_ref[:, j * MXU_N:(j + 1) * MXU_N],
                            cos, sin, 1.0 / HEAD_DIM)
        return jnp.concatenate([lo, hi], axis=1)

    qa_ref[:, 0:MXU_N] = nr_block(0).astype(_bf16)
    qa_ref[:, MXU_N:2 * MXU_N] = nr_block(1).astype(_bf16)
    kd_ref[...] = nr_block(2).astype(_bf16)
    b_refs = ((bq0_ref, bk0_ref, bv0_ref), (bq1_ref, bk1_ref, bv1_ref), (bq2_ref, bk2_ref, bv2_ref))
    for g, (_, r) in enumerate(B_CONFIGS):
        store_folded(b_refs[g][0], nr_block(3 + 2 * g), r)
        store_folded(b_refs[g][1], nr_block(4 + 2 * g), r)
    vd_ref[...] = matmul_block(NR_BLOCKS).astype(_bf16)
    for g, (_, r) in enumerate(B_CONFIGS):
        store_folded(b_refs[g][2], matmul_block(NR_BLOCKS + 1 + g), r)


def _proj0(x, seq, norm_g, w, head_gain, gm, cos, sin):
    t_rows = x.shape[0]
    tm = TM
    n_pos_blocks = seq // tm
    row = lambda i: (i, 0)
    out_shapes = [jax.ShapeDtypeStruct((t_rows, A_Q_DIM), _bf16),
                  jax.ShapeDtypeStruct((t_rows, MXU_N), _bf16),
                  jax.ShapeDtypeStruct((t_rows, MXU_N), _bf16)]
    out_specs = [pl.BlockSpec((tm, A_Q_DIM), row), pl.BlockSpec((tm, MXU_N), row),
                 pl.BlockSpec((tm, MXU_N), row)]
    for _, r in B_CONFIGS:
        for _ in range(3):
            out_shapes.append(jax.ShapeDtypeStruct((t_rows // r, r * B_DIM), _bf16))
            out_specs.append(pl.BlockSpec((tm // r, r * B_DIM), row))
    return pl.pallas_call(
        _proj0_kernel,
        name="proj0",
        grid=(t_rows // tm,),
        in_specs=[pl.BlockSpec((tm, D_MODEL), row),
                  _const_spec((1, D_MODEL)),
                  _const_spec((D_MODEL, P0_COLS)),
                  _const_spec((1, NR_BLOCKS * MXU_N)),
                  _const_spec((MXU_N, MXU_N)),
                  pl.BlockSpec((tm, LANES), lambda i: (i % n_pos_blocks, 0)),
                  pl.BlockSpec((tm, LANES), lambda i: (i % n_pos_blocks, 0))],
        out_specs=out_specs,
        out_shape=out_shapes,
        scratch_shapes=[pltpu.VMEM((tm, D_MODEL), _bf16), pltpu.VMEM((2, tm, LANES), _f32)],
        compiler_params=_params(1),
    )(x, norm_g, w, head_gain, gm, cos, sin)


def _local_attn_kernel(sink_ref, q_ref, k_ref, v_ref, *refs, seq, tq, half, group, has_sink, want_lse):
    o_ref = refs[0]
    bias_scr = refs[-1]
    win = tq + 2 * half
    lane = lax.broadcasted_iota(jnp.int32, (1, LANES), 1)
    first_qk = ((lane // (HEAD_DIM // 2)) % 2) == 0
    first_v = lane < HEAD_DIM
    row = lax.broadcasted_iota(jnp.int32, (2 * tq, 1), 0)
    top = row < tq
    qoff = jnp.where(top, row, row - tq)
    koff = lax.broadcasted_iota(jnp.int32, (1, win), 1)
    for e in range(3):
        bias_scr[e] = jnp.where(jnp.abs(e * half + qoff - koff) <= half, 0.0, NEG)

    def step(g, carry, lanes, sink):
        tiles = range(n_unroll)
        qs = [pl.multiple_of((g * n_unroll + u) * tq, tq) for u in tiles]
        ws = [pl.multiple_of(jnp.clip(qs[u] - half, 0, seq - win), half) for u in tiles]
        s = []
        for u in tiles:
            q2 = q_ref[0, pl.ds(qs[u], tq), lanes]
            zero = jnp.zeros_like(q2)
            qq = jnp.concatenate([jnp.where(first_qk, q2, zero), jnp.where(first_qk, zero, q2)], axis=0)
            kw = k_ref[0, pl.ds(ws[u], win), lanes]
            s.append(lax.dot_general(qq, kw, (((1,), (1,)), ((), ())), preferred_element_type=_f32))
        s = [s[u] + bias_scr[(qs[u] - ws[u]) // half] for u in tiles]
        m = [jnp.max(s[u], axis=-1, keepdims=True) for u in tiles]
        if has_sink:
            m = [jnp.maximum(m[u], sink) for u in tiles]
        p = [jnp.exp2(s[u] - m[u]) for u in tiles]
        den = [jnp.sum(p[u], axis=-1, keepdims=True) for u in tiles]
        if has_sink:
            den = [den[u] + jnp.exp2(sink - m[u]) for u in tiles]
        pv = [jnp.dot(p[u].astype(_bf16), v_ref[0, pl.ds(ws[u], win), lanes], preferred_element_type=_f32)
              for u in tiles]
        for u in tiles:
            od = pv[u] / den[u]
            o_ref[0, pl.ds(qs[u], tq), lanes] = jnp.where(first_v, od[:tq], od[tq:]).astype(o_ref.dtype)
            if want_lse:
                lse = m[u] + jnp.log2(den[u])
                refs[1][0, pl.ds(qs[u], tq), lanes] = jnp.where(first_v, lse[:tq], lse[tq:])
        return carry

    n_unroll = min(ATT_UNROLL, seq // tq)
    for pp in range(group):
        sink = None
        if has_sink:
            pair = pl.program_id(1) * group + pp
            sink = jnp.where(top, sink_ref[pair, 0], sink_ref[pair, 1]) * LOG2_E
        lax.fori_loop(0, seq // (tq * n_unroll),
                      functools.partial(step, lanes=slice(pp * LANES, (pp + 1) * LANES), sink=sink), 0)


def _local_attn(q, k, v, sink, *, half, kv_of_pair, want_lse):
    bsz, seq, width = q.shape
    n_pairs = width // LANES
    tq = ATT_TQ
    group = 1
    if kv_of_pair is None:
        kv_of_pair = lambda p: p
        while group < ATT_MAX_GROUP and n_pairs % (2 * group) == 0 and 2 * group * seq * LANES * 2 <= ATT_STEP_BYTES:
            group *= 2
    blk = group * LANES
    assert seq % (tq * min(ATT_UNROLL, seq // tq)) == 0 and seq >= tq + 2 * half
    has_sink = sink is not None
    if not has_sink:
        sink = jnp.zeros((1, 2), _f32)
    qmap = lambda b, p: (b, 0, p)
    kmap = lambda b, p: (b, 0, kv_of_pair(p))
    out_shape = [jax.ShapeDtypeStruct((bsz, seq, width), _bf16)]
    out_specs = [pl.BlockSpec((1, seq, blk), qmap)]
    if want_lse:
        out_shape.append(jax.ShapeDtypeStruct((bsz, seq, width), _f32))
        out_specs.append(pl.BlockSpec((1, seq, blk), qmap))
    return pl.pallas_call(
        functools.partial(_local_attn_kernel, seq=seq, tq=tq, half=half, group=group, has_sink=has_sink,
                          want_lse=want_lse),
        name="local_attn_h%d" % half,
        grid=(bsz, n_pairs // group),
        in_specs=[pl.BlockSpec(memory_space=pltpu.SMEM),
                  pl.BlockSpec((1, seq, blk), qmap),
                  pl.BlockSpec((1, seq, blk), kmap),
                  pl.BlockSpec((1, seq, blk), kmap)],
        out_specs=out_specs,
        out_shape=out_shape,
        scratch_shapes=[pltpu.VMEM((3, 2 * tq, tq + 2 * half), _f32)],
        compiler_params=_params(2),
    )(sink, q, k, v)


def _out0_kernel(x_ref, a_ref, o0_ref, l0_ref, o1_ref, l1_ref, o2_ref, l2_ref, wa_ref, wb_ref,
                 y_ref, o_scr, l_scr):
    tm = x_ref.shape[0]

    def unfold(src_ref, dst_scr, r):
        if r == 1:
            return src_ref[...].astype(_f32)
        for c in range(r):
            for t in range(2):
                lo = c * MXU_N + t * LANES
                dst_scr[t, pl.ds(c, tm // r, stride=r), :] = src_ref[:, lo:lo + LANES].astype(_f32)
        return jnp.concatenate([dst_scr[0], dst_scr[1]], axis=1)

    outs, lses = [], []
    for (o_ref, l_ref), (_, r) in zip(((o0_ref, l0_ref), (o1_ref, l1_ref), (o2_ref, l2_ref)), B_CONFIGS):
        outs.append(unfold(o_ref, o_scr, r))
        lses.append(unfold(l_ref, l_scr, r))
    m = jnp.maximum(jnp.maximum(lses[0], lses[1]), lses[2])
    es = [jnp.exp2(l - m) for l in lses]
    tot = es[0] + es[1] + es[2]
    b_out = (es[0] / tot) * outs[0] + (es[1] / tot) * outs[1] + (es[2] / tot) * outs[2]
    y = x_ref[...] + jnp.dot(a_ref[...], wa_ref[...], preferred_element_type=_f32)
    y_ref[...] = y + jnp.dot(b_out.astype(_bf16), wb_ref[...], preferred_element_type=_f32)


def _out0(x, a_out, b_outs, b_lses, wa, wb):
    t_rows = x.shape[0]
    tm = TM
    row = lambda i: (i, 0)
    in_specs = [pl.BlockSpec((tm, D_MODEL), row), pl.BlockSpec((tm, A_Q_DIM), row)]
    args = [x, a_out]
    for (_, r), o, l in zip(B_CONFIGS, b_outs, b_lses):
        in_specs += [pl.BlockSpec((tm // r, r * B_DIM), row)] * 2
        args += [o, l]
    in_specs += [_const_spec((A_Q_DIM, D_MODEL)), _const_spec((B_DIM, D_MODEL))]
    args += [wa, wb]
    return pl.pallas_call(
        _out0_kernel,
        name="out0",
        grid=(t_rows // tm,),
        in_specs=in_specs,
        out_specs=pl.BlockSpec((tm, D_MODEL), row),
        out_shape=jax.ShapeDtypeStruct((t_rows, D_MODEL), _f32),
        scratch_shapes=[pltpu.VMEM((2, tm, LANES), _f32), pltpu.VMEM((2, tm, LANES), _f32)],
        compiler_params=_params(1),
    )(*args)


def _ffn_kernel(xp_ref, x_ref, xn_ref, g_ref, wg_ref, wv_ref, cw_ref, wd_ref, y_ref,
                h_scr, gate_a, gate_b, val_a, val_b, acc_scr, *, tiles_per_seq):
    tm = x_ref.shape[0]
    i = pl.program_id(0)
    pos = i % tiles_per_seq
    keep_prev = jnp.where(pos == 0, 0.0, 1.0)
    keep_next = jnp.where(pos == tiles_per_seq - 1, 0.0, 1.0)
    g = g_ref[...]
    h_scr[0:HALO, :] = _rms(xp_ref[...], g).astype(_bf16)
    h_scr[HALO:HALO + tm, :] = _rms(x_ref[...], g).astype(_bf16)
    h_scr[HALO + tm:, :] = _rms(xn_ref[...], g).astype(_bf16)

    def up(j, gate_scr, val_scr):
        gate_scr[...] = jnp.dot(h_scr[...], wg_ref[j], preferred_element_type=_f32)
        gate_scr[HALO - F32_ROWS:HALO, :] = gate_scr[HALO - F32_ROWS:HALO, :] * keep_prev
        gate_scr[HALO + tm:HALO + tm + F32_ROWS, :] = gate_scr[HALO + tm:HALO + tm + F32_ROWS, :] * keep_next
        val_scr[...] = jnp.dot(h_scr[HALO:HALO + tm, :], wv_ref[j], preferred_element_type=_f32)

    def act(j, gate_scr, val_scr):
        cw = cw_ref[j]
        conv = (cw[0:1] * gate_scr[HALO - 1:HALO - 1 + tm, :]
                + cw[1:2] * gate_scr[HALO:HALO + tm, :]
                + cw[2:3] * gate_scr[HALO + 1:HALO + 1 + tm, :]
                + cw[3:4])
        return (jax.nn.gelu(conv) * val_scr[...]).astype(_bf16)

    bufs = ((gate_a, val_a), (gate_b, val_b))

    def run_chunks(j0, count, up_following):
        total = None
        for u in range(count):
            if u < count - 1 or up_following:
                up(j0 + u + 1, *bufs[(u + 1) % 2])
            a = act(j0 + u, *bufs[u % 2])
            d = jnp.dot(a, wd_ref[j0 + u], preferred_element_type=_f32)
            total = d if total is None else total + d
        return total

    def step(jj, carry):
        acc_scr[...] += run_chunks(FF_UNROLL * jj, FF_UNROLL, True)
        return carry

    n_loop = (N_FF_CHUNK - 1) // FF_UNROLL
    acc_scr[...] = jnp.zeros_like(acc_scr)
    up(0, gate_a, val_a)
    lax.fori_loop(0, n_loop, step, 0)
    tail = run_chunks(n_loop * FF_UNROLL, N_FF_CHUNK - n_loop * FF_UNROLL, False)
    y_ref[...] = x_ref[...] + (acc_scr[...] + tail)


def _ffn(x, seq, norm_g, wg, wv, cw, wd):
    t_rows = x.shape[0]
    tm = TM
    per = tm // HALO
    n_halo_blocks = t_rows // HALO
    return pl.pallas_call(
        functools.partial(_ffn_kernel, tiles_per_seq=seq // tm),
        name="conv_ffn",
        grid=(t_rows // tm,),
        in_specs=[pl.BlockSpec((HALO, D_MODEL), lambda i: (jnp.maximum(i * per - 1, 0), 0)),
                  pl.BlockSpec((tm, D_MODEL), lambda i: (i, 0)),
                  pl.BlockSpec((HALO, D_MODEL), lambda i: (jnp.minimum((i + 1) * per, n_halo_blocks - 1), 0)),
                  _const_spec((1, D_MODEL)),
                  _const_spec((N_FF_CHUNK, D_MODEL, FF_CHUNK)),
                  _const_spec((N_FF_CHUNK, D_MODEL, FF_CHUNK)),
                  _const_spec((N_FF_CHUNK, F32_ROWS, FF_CHUNK)),
                  _const_spec((N_FF_CHUNK, FF_CHUNK, D_MODEL))],
        out_specs=pl.BlockSpec((tm, D_MODEL), lambda i: (i, 0)),
        out_shape=jax.ShapeDtypeStruct((t_rows, D_MODEL), _f32),
        scratch_shapes=[pltpu.VMEM((tm + 2 * HALO, D_MODEL), _bf16),
                        pltpu.VMEM((tm + 2 * HALO, FF_CHUNK), _f32),
                        pltpu.VMEM((tm + 2 * HALO, FF_CHUNK), _f32),
                        pltpu.VMEM((tm, FF_CHUNK), _f32),
                        pltpu.VMEM((tm, FF_CHUNK), _f32),
                        pltpu.VMEM((tm, D_MODEL), _f32)],
        compiler_params=_params(1),
    )(x, x, x, norm_g, wg, wv, cw, wd)


def _proj1_kernel(x_ref, g_ref, win_ref, qlg_ref, kvg_ref, wq_ref, wk_ref, wvt_ref, qg_ref, kg_ref,
                  gm_ref, cos_ref, sin_ref, q_ref, k_ref, vt_ref):
    tm = x_ref.shape[0]
    h = _rms(x_ref[...], g_ref[...]).astype(_bf16)
    proj = jnp.dot(h, win_ref[...], preferred_element_type=_f32)
    cq = _rms(proj[:, :C_Q_LORA], qlg_ref[...]).astype(_bf16)
    ckv_f32 = _rms(proj[:, C_Q_LORA:C_Q_LORA + C_KV_LORA], kvg_ref[...])
    ckv = ckv_f32.astype(_bf16)
    ckv_t = ckv_f32.T.astype(_bf16)
    k_rope = proj[:, C_Q_LORA + C_KV_LORA:]
    k_rope2 = jnp.concatenate([k_rope, k_rope], axis=1)
    cos = cos_ref[...]
    sin = sin_ref[...]
    gm = gm_ref[...]
    qg = jnp.concatenate([qg_ref[...]] * 2, axis=1)
    kg = jnp.concatenate([kg_ref[...]] * 2, axis=1)
    for b in range(C_HEADS // 2):
        cols = slice(b * MXU_N, (b + 1) * MXU_N)
        aq = jnp.dot(cq, wq_ref[:, cols], preferred_element_type=_f32)
        lo, hi = _norm_rope(aq, gm, qg, cos, sin, 1.0 / C_QK)
        q_ref[:, cols] = jnp.concatenate([lo, hi], axis=1).astype(_bf16)
        ak = jnp.dot(ckv, wk_ref[:, cols], preferred_element_type=_f32) + k_rope2
        lo, hi = _norm_rope(ak, gm, kg, cos, sin, 1.0 / C_QK)
        k_ref[:, cols] = jnp.concatenate([lo, hi], axis=1).astype(_bf16)
    r = lax.broadcasted_iota(jnp.int32, (MXU_N, tm), 0)
    for b in range(C_HEADS * C_VROWS // MXU_N):
        rows = slice(b * MXU_N, (b + 1) * MXU_N)
        vt = jnp.dot(wvt_ref[rows, :], ckv_t, preferred_element_type=_f32)
        ones_rows = (r + b * MXU_N) % C_VROWS >= C_V
        vt_ref[0, rows, :] = jnp.where(ones_rows, 1.0, vt).astype(_bf16)


def _proj1(x, seq, norm_g, win, qlg, kvg, wq, wk, wvt, qg, kg, gm, cos, sin):
    t_rows = x.shape[0]
    tm = TM
    n_pos_blocks = seq // tm
    row = lambda i: (i, 0)
    qk_w = C_HEADS * LANES
    v_w = C_HEADS * C_VROWS
    return pl.pallas_call(
        _proj1_kernel,
        name="proj1",
        grid=(t_rows // tm,),
        in_specs=[pl.BlockSpec((tm, D_MODEL), row),
                  _const_spec((1, D_MODEL)),
                  _const_spec(win.shape),
                  _const_spec((1, C_Q_LORA)), _const_spec((1, C_KV_LORA)),
                  _const_spec((C_Q_LORA, qk_w)), _const_spec((C_KV_LORA, qk_w)),
                  _const_spec((v_w, C_KV_LORA)),
                  _const_spec((1, LANES)), _const_spec((1, LANES)),
                  _const_spec((MXU_N, MXU_N)),
                  pl.BlockSpec((tm, LANES), lambda i: (i % n_pos_blocks, 0)),
                  pl.BlockSpec((tm, LANES), lambda i: (i % n_pos_blocks, 0))],
        out_specs=[pl.BlockSpec((tm, qk_w), row), pl.BlockSpec((tm, qk_w), row),
                   pl.BlockSpec((1, v_w, tm), lambda i: (i, 0, 0))],
        out_shape=[jax.ShapeDtypeStruct((t_rows, qk_w), _bf16),
                   jax.ShapeDtypeStruct((t_rows, qk_w), _bf16),
                   jax.ShapeDtypeStruct((t_rows // tm, v_w, tm), _bf16)],
        compiler_params=_params(1),
    )(x, norm_g, win, qlg, kvg, wq, wk, wvt, qg, kg, gm, cos, sin)


def _mla_kernel(q_ref, k_ref, vt_ref, o_ref, sa_ref, sb_ref, *, seq, tk, unroll):
    tq = q_ref.shape[1]
    n_tiles = seq // tk
    s_bufs = (sa_ref, sb_ref)

    def scores(t, buf, hh):
        ks = pl.multiple_of(t * tk, tk)
        kt = k_ref[0, pl.ds(ks, tk), hh * LANES:(hh + 1) * LANES]
        qh = q_ref[0, :, hh * LANES:(hh + 1) * LANES]
        s = lax.dot_general(kt, qh, (((1,), (1,)), ((), ())), preferred_element_type=_f32)
        buf[hh] = s
        return jnp.max(s, axis=0, keepdims=True)

    def consume(t, buf, hh, tmax, state):
        m_old, acc = state
        m_new = jnp.maximum(m_old, tmax)
        alpha = jnp.exp2(m_old - m_new)
        p = jnp.exp2(buf[hh] - m_new).astype(_bf16)
        vt = vt_ref[t, hh * C_VROWS:(hh + 1) * C_VROWS, :]
        acc = alpha * acc + jnp.dot(vt, p, preferred_element_type=_f32)
        return m_new, acc

    def run_tiles(t0, tmax, state, score_following):
        tmax, state = list(tmax), list(state)
        for u in range(unroll):
            score_next = u < unroll - 1 or score_following
            cur, nxt = s_bufs[u % 2], s_bufs[(u + 1) % 2]
            for hh in range(2):
                if score_next:
                    tmax_next = scores(t0 + u + 1, nxt, hh)
                state[hh] = consume(t0 + u, cur, hh, tmax[hh], state[hh])
                if score_next:
                    tmax[hh] = tmax_next
        return tuple(tmax), tuple(state)

    state = tuple((jnp.full((1, tq), -jnp.inf, _f32), jnp.zeros((C_VROWS, tq), _f32)) for _ in range(2))
    tmax = tuple(scores(0, s_bufs[0], hh) for hh in range(2))
    tmax, state = lax.fori_loop(
        0, n_tiles // unroll - 1,
        lambda jj, c: run_tiles(unroll * jj, c[0], c[1], True), (tmax, state))
    _, ((_, acc0), (_, acc1)) = run_tiles(n_tiles - unroll, tmax, state, False)
    out_t = jnp.concatenate([acc0[:C_V] / acc0[C_V:C_V + 1], acc1[:C_V] / acc1[C_V:C_V + 1]], axis=0)
    o_ref[0] = out_t.T.astype(o_ref.dtype)


def _mla_attn(q, k, vt):
    bsz, seq, _ = q.shape
    n_pairs = C_HEADS // 2
    tq, tk = MLA_TQ, MLA_TK
    n_tiles = seq // tk
    unroll = MLA_UNROLL if n_tiles >= 3 * MLA_UNROLL else 2
    assert vt.shape[2] == tk and n_tiles % unroll == 0 and seq % tq == 0
    return pl.pallas_call(
        functools.partial(_mla_kernel, seq=seq, tk=tk, unroll=unroll),
        name="mla_attn",
        grid=(bsz, n_pairs, seq // tq),
        in_specs=[pl.BlockSpec((1, tq, 2 * LANES), lambda b, p, i: (b, i, p)),
                  pl.BlockSpec((1, seq, 2 * LANES), lambda b, p, i: (b, 0, p)),
                  pl.BlockSpec((seq // tk, 2 * C_VROWS, tk), lambda b, p, i: (b, p, 0))],
        out_specs=pl.BlockSpec((1, tq, LANES), lambda b, p, i: (b, i, p)),
        out_shape=jax.ShapeDtypeStruct((bsz, seq, C_HEADS * C_V), _bf16),
        scratch_shapes=[pltpu.VMEM((2, tk, tq), _f32), pltpu.VMEM((2, tk, tq), _f32)],
        compiler_params=_params(3),
    )(q, k, vt)


def _out1_kernel(x_ref, o_ref, w_ref, y_ref):
    y_ref[...] = x_ref[...] + jnp.dot(o_ref[...], w_ref[...], preferred_element_type=_f32)


def _out1(x, o, w):
    t_rows = x.shape[0]
    tm = TM
    row = lambda i: (i, 0)
    return pl.pallas_call(
        _out1_kernel,
        name="out1",
        grid=(t_rows // tm,),
        in_specs=[pl.BlockSpec((tm, D_MODEL), row), pl.BlockSpec((tm, o.shape[1]), row),
                  _const_spec(w.shape)],
        out_specs=pl.BlockSpec((tm, D_MODEL), row),
        out_shape=jax.ShapeDtypeStruct((t_rows, D_MODEL), _f32),
        compiler_params=_params(1),
    )(x, o, w)


def _pair_cols(a, b):
    r = np.arange(HEAD_DIM // 2)
    return np.concatenate([a + r, b + r, a + HEAD_DIM // 2 + r, b + HEAD_DIM // 2 + r])


_PAIR_DIMS = _pair_cols(0, 0)


def _layer0_columns():
    nr, val = [], []
    for p in range(A_Q_HEADS // 2):
        nr.append(_pair_cols(2 * p * HEAD_DIM, (2 * p + 1) * HEAD_DIM))
    for h in range(A_KV_HEADS):
        nr.append(_pair_cols(A_Q_DIM + h * HEAD_DIM, A_Q_DIM + h * HEAD_DIM))
    b0 = A_Q_DIM + 2 * A_KV_DIM
    for g in range(N_BRANCH):
        for t in range(2):
            base = b0 + g * 3 * B_DIM + t * B_DIM
            for p in range(B_HEADS // 2):
                nr.append(_pair_cols(base + 2 * p * HEAD_DIM, base + (2 * p + 1) * HEAD_DIM))
    d = np.arange(HEAD_DIM)
    for h in range(A_KV_HEADS):
        base = A_Q_DIM + A_KV_DIM + h * HEAD_DIM
        val.append(np.concatenate([base + d, base + d]))
    for g in range(N_BRANCH):
        val.append(b0 + g * 3 * B_DIM + 2 * B_DIM + np.arange(B_DIM))
    return np.concatenate(nr + val)


def _layer0_gains(a_q_gain, a_k_gain, b_q_gain, b_k_gain):
    scale = (HEAD_DIM ** -0.5) * LOG2_E
    aq = a_q_gain[_PAIR_DIMS] * scale
    ak = a_k_gain[_PAIR_DIMS]
    parts = [aq] * (A_Q_HEADS // 2) + [ak] * A_KV_HEADS
    for g in range(N_BRANCH):
        parts += [b_q_gain[g][_PAIR_DIMS] * scale] * (B_HEADS // 2)
        parts += [b_k_gain[g][_PAIR_DIMS]] * (B_HEADS // 2)
    return jnp.concatenate(parts)[None, :]


def _layer0_group_matrix():
    lane = np.arange(MXU_N)
    head = (lane // LANES) * 2 + (lane // (HEAD_DIM // 2)) % 2
    return jnp.asarray(head[:, None] == head[None, :], _bf16)


def _rope_tables(seq, dim):
    inv = jnp.power(ROPE_THETA, -jnp.arange(0, dim, 2, dtype=_f32) / dim)
    ang = jnp.arange(seq, dtype=_f32)[:, None] * inv[None, :]
    return jnp.cos(ang), jnp.sin(ang)


def _layer0_rope(seq):
    cos, sin = _rope_tables(seq, HEAD_DIM)
    return (jnp.concatenate([cos, cos, cos, cos], axis=1),
            jnp.concatenate([-sin, -sin, sin, sin], axis=1))


_C_HALF = C_ROPE // 2
_C_LANE_DIM = np.full(LANES, -1)
_C_NOPE_HALF = C_NOPE // 2
_HALF_BLOCK = LANES // 2
_C_LANE_DIM[0:_C_HALF] = C_NOPE + np.arange(_C_HALF)
_C_LANE_DIM[_C_HALF:_C_HALF + _C_NOPE_HALF] = np.arange(_C_NOPE_HALF)
_C_LANE_DIM[_HALF_BLOCK:_HALF_BLOCK + _C_HALF] = C_NOPE + _C_HALF + np.arange(_C_HALF)
_C_LANE_DIM[_HALF_BLOCK + _C_HALF:_HALF_BLOCK + _C_HALF + _C_NOPE_HALF] = _C_NOPE_HALF + np.arange(_C_NOPE_HALF)


def _place_heads(w, per_head, lane_dim):
    idx = (np.arange(C_HEADS)[:, None] * per_head + np.maximum(lane_dim, 0)[None, :]).reshape(-1)
    keep = np.tile(lane_dim >= 0, C_HEADS)
    return jnp.where(jnp.asarray(keep)[None, :], w[:, idx], 0.0)


def _layer1_rope(seq):
    cos, sin = _rope_tables(seq, C_ROPE)
    ones = jnp.ones((seq, _HALF_BLOCK - _C_HALF), _f32)
    zeros = jnp.zeros((seq, _HALF_BLOCK - _C_HALF), _f32)
    return (jnp.concatenate([cos, ones, cos, ones], axis=1),
            jnp.concatenate([-sin, zeros, sin, zeros], axis=1))


def _layer1_group_matrix():
    lane = np.arange(MXU_N)
    return jnp.asarray((lane[:, None] // LANES) == (lane[None, :] // LANES), _bf16)


def _trunk(x3, e_norm, e_w_in, e_a_q_gain, e_a_k_gain, e_a_sink, e_b_q_gain, e_b_k_gain, e_w_out,
           o_norm, o_w_in, o_q_lora_gain, o_w_uq, o_kv_gain, o_w_ukv, o_q_gain, o_k_gain, o_w_out,
           f_norm, f_w_up, f_conv_w, f_conv_b, f_w_down):
    bsz, seq, _ = x3.shape
    assert seq % TM == 0
    x = x3.reshape(bsz * seq, D_MODEL)
    for layer in range(DEPTH):
        i = layer // 2
        if layer % 2 == 0:
            w0 = e_w_in[i][:, _layer0_columns()].astype(_bf16)
            cos, sin = _layer0_rope(seq)
            outs = _proj0(x, seq, e_norm[i][None, :], w0,
                          _layer0_gains(e_a_q_gain[i], e_a_k_gain[i], e_b_q_gain[i], e_b_k_gain[i]),
                          _layer0_group_matrix(), cos, sin)
            qa, kd, vd = outs[:3]
            a_out = _local_attn(qa.reshape(bsz, seq, A_Q_DIM), kd.reshape(bsz, seq, MXU_N),
                                vd.reshape(bsz, seq, MXU_N), e_a_sink[i].reshape(A_Q_HEADS // 2, 2),
                                half=A_HALF_WINDOW, kv_of_pair=lambda p: p // 2, want_lse=False)[0]
            b_outs, b_lses = [], []
            for g, (window, r) in enumerate(B_CONFIGS):
                bq, bk, bv = (t.reshape(bsz, seq // r, r * B_DIM) for t in outs[3 + 3 * g:6 + 3 * g])
                o, lse = _local_attn(bq, bk, bv, None, half=(window // 2) // r,
                                     kv_of_pair=None, want_lse=True)
                b_outs.append(o.reshape(bsz * seq // r, r * B_DIM))
                b_lses.append(lse.reshape(bsz * seq // r, r * B_DIM))
            w_out = e_w_out[i].astype(_bf16)
            x = _out0(x, a_out.reshape(bsz * seq, A_Q_DIM), b_outs, b_lses,
                      w_out[:A_Q_DIM], w_out[A_Q_DIM:])
        else:
            w_in = o_w_in[i]
            k_rope_cols = jnp.zeros((D_MODEL, LANES), _f32)
            k_rope_cols = k_rope_cols.at[:, 0:_C_HALF].set(w_in[:, C_Q_LORA + C_KV_LORA:C_Q_LORA + C_KV_LORA + _C_HALF])
            k_rope_cols = k_rope_cols.at[:, _HALF_BLOCK:_HALF_BLOCK + _C_HALF].set(
                w_in[:, C_Q_LORA + C_KV_LORA + _C_HALF:])
            win = jnp.concatenate([w_in[:, :C_Q_LORA + C_KV_LORA], k_rope_cols], axis=1).astype(_bf16)
            wq = _place_heads(o_w_uq[i], C_QK, _C_LANE_DIM).astype(_bf16)
            nope_only = np.where(_C_LANE_DIM < C_NOPE, _C_LANE_DIM, -1)
            wk = _place_heads(o_w_ukv[i], C_NOPE + C_V, nope_only).astype(_bf16)
            r_in_head = np.arange(C_HEADS * C_VROWS) % C_VROWS
            head = np.arange(C_HEADS * C_VROWS) // C_VROWS
            v_col = head * (C_NOPE + C_V) + C_NOPE + np.minimum(r_in_head, C_V - 1)
            wvt = jnp.where(jnp.asarray(r_in_head < C_V)[:, None], o_w_ukv[i][:, v_col].T, 0.0).astype(_bf16)
            lane_ok = jnp.asarray(_C_LANE_DIM >= 0)
            q_scale = (C_QK ** -0.5) * LOG2_E
            qg = jnp.where(lane_ok, o_q_gain[i][np.maximum(_C_LANE_DIM, 0)] * q_scale, 0.0)[None, :]
            kg = jnp.where(lane_ok, o_k_gain[i][np.maximum(_C_LANE_DIM, 0)], 0.0)[None, :]
            cos, sin = _layer1_rope(seq)
            q, k, vt = _proj1(x, seq, o_norm[i][None, :], win, o_q_lora_gain[i][None, :],
                              o_kv_gain[i][None, :], wq, wk, wvt, qg, kg, _layer1_group_matrix(),
                              cos, sin)
            o = _mla_attn(q.reshape(bsz, seq, -1), k.reshape(bsz, seq, -1), vt)
            x = _out1(x, o.reshape(bsz * seq, C_HEADS * C_V), o_w_out[i].astype(_bf16))
        w_up = f_w_up[layer]
        wg = w_up[:, :D_FF].reshape(D_MODEL, N_FF_CHUNK, FF_CHUNK).transpose(1, 0, 2).astype(_bf16)
        wv_ = w_up[:, D_FF:].reshape(D_MODEL, N_FF_CHUNK, FF_CHUNK).transpose(1, 0, 2).astype(_bf16)
        cw = jnp.concatenate([f_conv_w[layer], f_conv_b[layer][None, :],
                              jnp.zeros((F32_ROWS - CONV_W - 1, D_FF), _f32)], axis=0)
        cw = cw.reshape(F32_ROWS, N_FF_CHUNK, FF_CHUNK).transpose(1, 0, 2)
        wd = f_w_down[layer].reshape(N_FF_CHUNK, FF_CHUNK, D_MODEL).astype(_bf16)
        x = _ffn(x, seq, f_norm[layer][None, :], wg, wv_, cw, wd)
    return x.reshape(bsz, seq, D_MODEL)


def kernel(x_prompt, x_sample, e_norm, e_w_in, e_a_q_gain, e_a_k_gain, e_a_sink, e_b_q_gain, e_b_k_gain, e_w_out, o_norm, o_w_in, o_q_lora_gain, o_w_uq, o_kv_gain, o_w_ukv, o_q_gain, o_k_gain, o_w_out, f_norm, f_w_up, f_conv_w, f_conv_b, f_w_down):
    weights = (e_norm, e_w_in, e_a_q_gain, e_a_k_gain, e_a_sink, e_b_q_gain, e_b_k_gain, e_w_out,
               o_norm, o_w_in, o_q_lora_gain, o_w_uq, o_kv_gain, o_w_ukv, o_q_gain, o_k_gain, o_w_out,
               f_norm, f_w_up, f_conv_w, f_conv_b, f_w_down)
    return (_trunk(x_prompt, *weights), _trunk(x_sample, *weights))
```
